```python
import jax, jax.numpy as jnp
from jax import lax
import numpy as np

D_MODEL = 1024
BATCH = 8
SEQ = 2048
DEPTH = 4
DEC_BATCH = 128
DEC_SEQ = 4
PAST_LEN = 16384
PAGE_SIZE = 128

N_MIXERS = 2
N_CONV_LAYERS = (DEPTH + 1) // 2
N_MLSTM_LAYERS = DEPTH // 2
CONV_WIDTH = 31
D_FF = 4 * D_MODEL
ML_PROJ_FACTOR = 2
D_INNER = ML_PROJ_FACTOR * D_MODEL
ML_HEADS = 4
ML_HEAD_DIM = D_INNER // ML_HEADS
ML_QKV_BLOCK = 4
ML_CONV_WIDTH = 4
ML_CHUNK = 64
DN_ALPHA = (2.0 * DEPTH) ** 0.25
DN_BETA = (8.0 * DEPTH) ** -0.25
LN_EPS = 1e-5

kernel_name = 'conformer_conv_mlstm_hybrid_step'


def _layer_norm(x, g, b):
    xf = x.astype(jnp.float32)
    mu = jnp.mean(xf, axis=-1, keepdims=True)
    var = jnp.mean(jnp.square(xf - mu), axis=-1, keepdims=True)
    y = (xf - mu) * lax.rsqrt(var + LN_EPS) * g.astype(jnp.float32) + b.astype(jnp.float32)
    return y.astype(x.dtype)


def _causal_dwconv(x, buf, w, b):
    width = w.shape[0]
    xp = jnp.concatenate([buf.astype(x.dtype), x], axis=1)
    y = lax.conv_general_dilated(xp, w[:, None, :].astype(x.dtype), window_strides=(1,), padding='VALID',
                                 dimension_numbers=('NWC', 'WIO', 'NWC'), feature_group_count=x.shape[-1])
    return y + b, xp[:, xp.shape[1] - (width - 1):]


def _conv_module(x, buf, w_pw1, b_pw1, w_dw, b_dw, ln_g, ln_b, w_pw2, b_pw2):
    a, g = jnp.split(x @ w_pw1 + b_pw1, 2, axis=-1)
    u = a * jax.nn.sigmoid(g)
    c, new_buf = _causal_dwconv(u, buf, w_dw, b_dw)
    c = jax.nn.silu(_layer_norm(c, ln_g, ln_b))
    return c @ w_pw2 + b_pw2, new_buf


def _headwise(x, w):
    xs = x.reshape(x.shape[:-1] + (w.shape[0], ML_QKV_BLOCK))
    return jnp.einsum('btgi,gio->btgo', xs, w).reshape(x.shape)


def _mlstm_cell(q, k, v, ig, lf, C, n, m):
    bsz, nh, T, dh = q.shape
    L = ML_CHUNK if T % ML_CHUNK == 0 else T
    nc = T // L

    def to_chunks(a):
        a = a.reshape(a.shape[:2] + (nc, L) + a.shape[3:])
        return jnp.moveaxis(a, 2, 0)

    causal = jnp.tril(jnp.ones((L, L), dtype=bool))

    def step(carry, inp):
        C, n, m = carry
        qc, kc, vc, ic, fc = inp
        b = jnp.cumsum(fc, axis=-1)
        d = jnp.where(causal, b[..., :, None] - b[..., None, :] + ic[..., None, :], -jnp.inf)
        inter = b + m[..., None]
        m_t = jnp.maximum(inter, jnp.max(d, axis=-1))
        s = jnp.einsum('bhtd,bhsd->bhts', qc, kc) * jnp.exp(d - m_t[..., None])
        a = jnp.exp(inter - m_t)
        num = a[..., None] * jnp.einsum('bhtd,bhdv->bhtv', qc, C) + jnp.einsum('bhts,bhsv->bhtv', s, vc)
        den = a * jnp.einsum('bhtd,bhd->bht', qc, n) + jnp.sum(s, axis=-1)
        h = num / jnp.maximum(jnp.abs(den), jnp.exp(-m_t))[..., None]
        g = b[..., -1:] - b + ic
        m_new = jnp.maximum(b[..., -1] + m, jnp.max(g, axis=-1))
        decay = jnp.exp(b[..., -1] + m - m_new)
        wk = kc * jnp.exp(g - m_new[..., None])[..., None]
        C_new = decay[..., None, None] * C + jnp.einsum('bhsd,bhsv->bhdv', wk, vc)
        n_new = decay[..., None] * n + jnp.sum(wk, axis=2)
        return (C_new, n_new, m_new), h

    (C, n, m), hs = lax.scan(step, (C, n, m), (to_chunks(q), to_chunks(k), to_chunks(v), to_chunks(ig), to_chunks(lf)))
    h = jnp.moveaxis(hs, 0, 2).reshape(bsz, nh, T, dh)
    return h, C, n, m


def _mlstm_layer(x, conv_buf, C, n, m, w_in, conv_w, conv_b, w_q, w_k, w_v, w_if, b_if, norm_g, skip, w_down):
    bsz, T, _ = x.shape
    xm, o_pre = jnp.split(x @ w_in, 2, axis=-1)
    xc, new_buf = _causal_dwconv(xm, conv_buf, conv_w, conv_b)
    xc = jax.nn.silu(xc)
    q = _headwise(xc, w_q)
    k = _headwise(xc, w_k)
    v = _headwise(xm, w_v)
    gates = (jnp.concatenate([q, k, v], axis=-1) @ w_if + b_if).astype(jnp.float32)
    ig = gates[..., :ML_HEADS].transpose(0, 2, 1)
    lf = jax.nn.log_sigmoid(gates[..., ML_HEADS:]).transpose(0, 2, 1)

    def heads(a):
        return a.reshape(bsz, T, ML_HEADS, ML_HEAD_DIM).transpose(0, 2, 1, 3).astype(jnp.float32)

    h, C, n, m = _mlstm_cell(heads(q), heads(k) * (ML_HEAD_DIM ** -0.5), heads(v), ig, lf,
                             C.astype(jnp.float32), n.astype(jnp.float32), m.astype(jnp.float32))
    mu = jnp.mean(h, axis=-1, keepdims=True)
    var = jnp.mean(jnp.square(h - mu), axis=-1, keepdims=True)
    hn = ((h - mu) * lax.rsqrt(var + LN_EPS)).transpose(0, 2, 1, 3).reshape(bsz, T, D_INNER)
    hn = hn * norm_g.astype(jnp.float32)
    out = jax.nn.sigmoid(o_pre) * (hn.astype(x.dtype) + skip * xc)
    return out @ w_down, new_buf, C, n, m


def _trunk(x, conv_st, mlc_st, C_st, n_st, m_st, p):
    conv_new, mlc_new, C_new, n_new, m_new = [], [], [], [], []
    for i in range(DEPTH):
        j = i // N_MIXERS
        if i % N_MIXERS == 0:
            mix, buf = _conv_module(x, conv_st[j], p['conv_w_pw1'][j], p['conv_b_pw1'][j], p['conv_w_dw'][j],
                                    p['conv_b_dw'][j], p['conv_ln_g'][j], p['conv_ln_b'][j],
                                    p['conv_w_pw2'][j], p['conv_b_pw2'][j])
            conv_new.append(buf)
        else:
            mix, buf, C, n, m = _mlstm_layer(x, mlc_st[j], C_st[j], n_st[j], m_st[j], p['ml_w_in'][j],
                                             p['ml_conv_w'][j], p['ml_conv_b'][j], p['ml_w_q'][j], p['ml_w_k'][j],
                                             p['ml_w_v'][j], p['ml_w_if'][j], p['ml_b_if'][j], p['ml_norm_g'][j],
                                             p['ml_skip'][j], p['ml_w_down'][j])
            mlc_new.append(buf)
            C_new.append(C)
            n_new.append(n)
            m_new.append(m)
        x = _layer_norm(DN_ALPHA * x + mix, p['ln_mix_g'][i], p['ln_mix_b'][i])
        hid = jnp.square(jax.nn.relu(x @ p['ffn_w1'][i]))
        x = _layer_norm(DN_ALPHA * x + hid @ p['ffn_w2'][i], p['ln_ffn_g'][i], p['ln_ffn_b'][i])
    return x, jnp.stack(conv_new), jnp.stack(mlc_new), jnp.stack(C_new), jnp.stack(n_new), jnp.stack(m_new)


def setup_inputs(seed: int = 0) -> dict:
    key = jax.random.key(seed)
    ks = iter(jax.random.split(key, 40))

    def nrm(shape, scale):
        return jax.random.normal(next(ks), shape, jnp.float32) * scale

    NC, NM = N_CONV_LAYERS, N_MLSTM_LAYERS
    return {
        'x_prompt': nrm((BATCH, SEQ, D_MODEL), 1.0),
        'x_sample': nrm((DEC_BATCH, DEC_SEQ, D_MODEL), 1.0),
        'state_conv': nrm((NC, DEC_BATCH, CONV_WIDTH - 1, D_MODEL), 0.5),
        'state_mlstm_conv': nrm((NM, DEC_BATCH, ML_CONV_WIDTH - 1, D_INNER), 0.5),
        'state_mlstm_C': nrm((NM, DEC_BATCH, ML_HEADS, ML_HEAD_DIM, ML_HEAD_DIM), 0.05),
        'state_mlstm_n': nrm((NM, DEC_BATCH, ML_HEADS, ML_HEAD_DIM), 0.1),
        'state_mlstm_m': nrm((NM, DEC_BATCH, ML_HEADS), 0.5),
        'conv_w_pw1': nrm((NC, D_MODEL, 2 * D_MODEL), D_MODEL ** -0.5),
        'conv_b_pw1': nrm((NC, 2 * D_MODEL), 0.01),
        'conv_w_dw': nrm((NC, CONV_WIDTH, D_MODEL), CONV_WIDTH ** -0.5),
        'conv_b_dw': nrm((NC, D_MODEL), 0.01),
        'conv_ln_g': 1.0 + nrm((NC, D_MODEL), 0.02),
        'conv_ln_b': nrm((NC, D_MODEL), 0.02),
        'conv_w_pw2': nrm((NC, D_MODEL, D_MODEL), DN_BETA * D_MODEL ** -0.5),
        'conv_b_pw2': nrm((NC, D_MODEL), 0.01),
        'ml_w_in': nrm((NM, D_MODEL, 2 * D_INNER), D_MODEL ** -0.5),
        'ml_conv_w': nrm((NM, ML_CONV_WIDTH, D_INNER), ML_CONV_WIDTH ** -0.5),
        'ml_conv_b': nrm((NM, D_INNER), 0.01),
        'ml_w_q': nrm((NM, D_INNER // ML_QKV_BLOCK, ML_QKV_BLOCK, ML_QKV_BLOCK), ML_QKV_BLOCK ** -0.5),
        'ml_w_k': nrm((NM, D_INNER // ML_QKV_BLOCK, ML_QKV_BLOCK, ML_QKV_BLOCK), ML_QKV_BLOCK ** -0.5),
        'ml_w_v': nrm((NM, D_INNER // ML_QKV_BLOCK, ML_QKV_BLOCK, ML_QKV_BLOCK), ML_QKV_BLOCK ** -0.5),
        'ml_w_if': nrm((NM, 3 * D_INNER, 2 * ML_HEADS), (3 * D_INNER) ** -0.5),
        'ml_b_if': jnp.concatenate([nrm((NM, ML_HEADS), 0.1), 3.0 + nrm((NM, ML_HEADS), 0.5)], axis=-1),
        'ml_norm_g': 1.0 + nrm((NM, D_INNER), 0.02),
        'ml_skip': 1.0 + nrm((NM, D_INNER), 0.02),
        'ml_w_down': nrm((NM, D_INNER, D_MODEL), DN_BETA * D_INNER ** -0.5),
        'ln_mix_g': 1.0 + nrm((DEPTH, D_MODEL), 0.02),
        'ln_mix_b': nrm((DEPTH, D_MODEL), 0.02),
        'ffn_w1': nrm((DEPTH, D_MODEL, D_FF), D_MODEL ** -0.5),
        'ffn_w2': nrm((DEPTH, D_FF, D_MODEL), DN_BETA * D_FF ** -0.5),
        'ln_ffn_g': 1.0 + nrm((DEPTH, D_MODEL), 0.02),
        'ln_ffn_b': nrm((DEPTH, D_MODEL), 0.02),
    }


def reference(x_prompt, x_sample, state_conv, state_mlstm_conv, state_mlstm_C, state_mlstm_n, state_mlstm_m,
              conv_w_pw1, conv_b_pw1, conv_w_dw, conv_b_dw, conv_ln_g, conv_ln_b, conv_w_pw2, conv_b_pw2,
              ml_w_in, ml_conv_w, ml_conv_b, ml_w_q, ml_w_k, ml_w_v, ml_w_if, ml_b_if, ml_norm_g, ml_skip, ml_w_down,
              ln_mix_g, ln_mix_b, ffn_w1, ffn_w2, ln_ffn_g, ln_ffn_b):
    p = dict(conv_w_pw1=conv_w_pw1, conv_b_pw1=conv_b_pw1, conv_w_dw=conv_w_dw, conv_b_dw=conv_b_dw,
             conv_ln_g=conv_ln_g, conv_ln_b=conv_ln_b, conv_w_pw2=conv_w_pw2, conv_b_pw2=conv_b_pw2,
             ml_w_in=ml_w_in, ml_conv_w=ml_conv_w, ml_conv_b=ml_conv_b, ml_w_q=ml_w_q, ml_w_k=ml_w_k,
             ml_w_v=ml_w_v, ml_w_if=ml_w_if, ml_b_if=ml_b_if, ml_norm_g=ml_norm_g, ml_skip=ml_skip,
             ml_w_down=ml_w_down, ln_mix_g=ln_mix_g, ln_mix_b=ln_mix_b, ffn_w1=ffn_w1, ffn_w2=ffn_w2,
             ln_ffn_g=ln_ffn_g, ln_ffn_b=ln_ffn_b)
    zc = jnp.zeros((N_CONV_LAYERS, BATCH, CONV_WIDTH - 1, D_MODEL), x_prompt.dtype)
    zmc = jnp.zeros((N_MLSTM_LAYERS, BATCH, ML_CONV_WIDTH - 1, D_INNER), x_prompt.dtype)
    zC = jnp.zeros((N_MLSTM_LAYERS, BATCH, ML_HEADS, ML_HEAD_DIM, ML_HEAD_DIM), jnp.float32)
    zn = jnp.zeros((N_MLSTM_LAYERS, BATCH, ML_HEADS, ML_HEAD_DIM), jnp.float32)
    zm = jnp.zeros((N_MLSTM_LAYERS, BATCH, ML_HEADS), jnp.float32)
    y_prompt, conv_p, mlc_p, C_p, n_p, m_p = _trunk(x_prompt, zc, zmc, zC, zn, zm, p)
    y_sample, conv_s, mlc_s, C_s, n_s, m_s = _trunk(x_sample, state_conv, state_mlstm_conv, state_mlstm_C,
                                                    state_mlstm_n, state_mlstm_m, p)
    return (y_prompt, y_sample, conv_p, conv_s, mlc_p, mlc_s, C_p, C_s, n_p, n_s, m_p, m_s)
```

```python
import functools

import jax
import jax.numpy as jnp
from jax import lax
from jax.experimental import pallas as pl
from jax.experimental.pallas import tpu as pltpu

D_MODEL = 1024
DEPTH = 4
CONV_WIDTH = 31
D_FF = 4 * D_MODEL
D_INNER = 2 * D_MODEL
ML_HEADS = 4
ML_HEAD_DIM = D_INNER // ML_HEADS
ML_QKV_BLOCK = 4
ML_CONV_WIDTH = 4
DN_ALPHA = (2.0 * DEPTH) ** 0.25
LN_EPS = 1e-5
K_SCALE = ML_HEAD_DIM ** -0.5

LANES = 128
SUBLANES = 8
GATE_LANES = LANES
VMEM_LIMIT = 56 * 1024 * 1024

FF_CHUNK = 1024
FFN_ROWS = 512
CONV_ROWS = 256
CONV_HEAD = 32
CONV_RCHUNK = 32
ML_CHUNK = 256
ML_HEADROWS = 8
S_BT = 32
S_TPAD = 8

BF16 = jnp.bfloat16
F32 = jnp.float32


def _const_spec(shape):
    nd = len(shape)
    return pl.BlockSpec(shape, lambda *_: (0,) * nd, pipeline_mode=pl.Buffered(1))


def _params(n_grid):
    return pltpu.CompilerParams(dimension_semantics=("arbitrary",) * n_grid,
                                vmem_limit_bytes=VMEM_LIMIT)


def _ln(x, g, b):
    mu = jnp.mean(x, axis=-1, keepdims=True)
    xc = x - mu
    var = jnp.mean(xc * xc, axis=-1, keepdims=True)
    return xc * lax.rsqrt(var + LN_EPS) * g + b


def _mm(a, w):
    return jnp.dot(a.astype(BF16), w, preferred_element_type=F32)


def _log_sigmoid(x):
    return jnp.minimum(x, 0.0) - jnp.log1p(jnp.exp(-jnp.abs(x)))


def _ffn_body(x_ref, w1_ref, w2_ref, g_ref, b_ref, o_ref):
    x = x_ref[...]
    xb = x.astype(BF16)
    acc = jnp.zeros_like(x)
    for j in range(D_FF // FF_CHUNK):
        cols = slice(j * FF_CHUNK, (j + 1) * FF_CHUNK)
        h = jnp.dot(xb, w1_ref[:, cols], preferred_element_type=F32)
        h = jnp.square(jnp.maximum(h, 0.0))
        acc = acc + jnp.dot(h.astype(BF16), w2_ref[cols, :], preferred_element_type=F32)
    o_ref[...] = _ln(DN_ALPHA * x + acc, g_ref[...], b_ref[...])


def _ffn(x2, w1, w2, g, b):
    n = x2.shape[0]
    rows = min(FFN_ROWS, n)
    return pl.pallas_call(
        _ffn_body,
        grid=(n // rows,),
        in_specs=[pl.BlockSpec((rows, D_MODEL), lambda i: (i, 0)),
                  _const_spec((D_MODEL, D_FF)), _const_spec((D_FF, D_MODEL)),
                  _const_spec((1, D_MODEL)), _const_spec((1, D_MODEL))],
        out_specs=pl.BlockSpec((rows, D_MODEL), lambda i: (i, 0)),
        out_shape=jax.ShapeDtypeStruct((n, D_MODEL), F32),
        compiler_params=_params(1),
        name="ffn",
    )(x2, w1, w2, g, b)


def _conv_prompt_body(x_ref, wpw1_ref, bpw1_ref, wdw_ref, bdw_ref, lng_ref, lnb_ref, wpw2_ref,
                      bpw2_ref, mg_ref, mb_ref, y_ref, st_ref, ush_ref, c_ref):
    t = pl.program_id(1)
    rows = CONV_ROWS
    span = rows + CONV_HEAD

    @pl.when(t == 0)
    def _():
        ush_ref[0, 0:CONV_HEAD, :] = jnp.zeros((CONV_HEAD, D_MODEL), F32)

    x = x_ref[0]
    z = _mm(x, wpw1_ref[...]) + bpw1_ref[...]
    u = z[:, :D_MODEL] * jax.nn.sigmoid(z[:, D_MODEL:])
    ush_ref[0, CONV_HEAD:span, :] = u
    for r in range(1, SUBLANES):
        ush_ref[r, 0:span - SUBLANES, :] = ush_ref[0, r:r + span - SUBLANES, :]

    first = CONV_HEAD - (CONV_WIDTH - 1)

    def chunk(i, carry):
        r0 = pl.multiple_of(i * CONV_RCHUNK, CONV_RCHUNK)
        acc = jnp.zeros((CONV_RCHUNK, D_MODEL), F32) + bdw_ref[...]
        for k in range(CONV_WIDTH):
            off = first + k
            a, r = divmod(off, SUBLANES)
            win = ush_ref[r, pl.ds(r0 + a * SUBLANES, CONV_RCHUNK), :]
            acc = acc + win * wdw_ref[k:k + 1, :]
        c = _ln(acc, lng_ref[...], lnb_ref[...])
        c = c * jax.nn.sigmoid(c)
        c_ref[pl.ds(r0, CONV_RCHUNK), :] = c.astype(BF16)
        return carry

    lax.fori_loop(0, rows // CONV_RCHUNK, chunk, 0)

    mix = jnp.dot(c_ref[...], wpw2_ref[...], preferred_element_type=F32) + bpw2_ref[...]
    y_ref[0] = _ln(DN_ALPHA * x + mix, mg_ref[...], mb_ref[...])

    @pl.when(t == pl.num_programs(1) - 1)
    def _():
        st_ref[0] = ush_ref[0, span - (CONV_WIDTH - 1):span, :]

    ush_ref[0, 0:CONV_HEAD, :] = ush_ref[0, rows:span, :]


def _conv_prompt(x, wpw1, bpw1, wdw, bdw, lng, lnb, wpw2, bpw2, mg, mb):
    bsz, seq, _ = x.shape
    rows = CONV_ROWS
    return pl.pallas_call(
        _conv_prompt_body,
        grid=(bsz, seq // rows),
        in_specs=[pl.BlockSpec((1, rows, D_MODEL), lambda b, t: (b, t, 0)),
                  _const_spec((D_MODEL, 2 * D_MODEL)), _const_spec((1, 2 * D_MODEL)),
                  _const_spec((CONV_WIDTH, D_MODEL)), _const_spec((1, D_MODEL)),
                  _const_spec((1, D_MODEL)), _const_spec((1, D_MODEL)),
                  _const_spec((D_MODEL, D_MODEL)), _const_spec((1, D_MODEL)),
                  _const_spec((1, D_MODEL)), _const_spec((1, D_MODEL))],
        out_specs=[pl.BlockSpec((1, rows, D_MODEL), lambda b, t: (b, t, 0)),
                   pl.BlockSpec((1, CONV_WIDTH - 1, D_MODEL), lambda b, t: (b, 0, 0))],
        out_shape=[jax.ShapeDtypeStruct((bsz, seq, D_MODEL), F32),
                   jax.ShapeDtypeStruct((bsz, CONV_WIDTH - 1, D_MODEL), F32)],
        scratch_shapes=[pltpu.VMEM((SUBLANES, rows + CONV_HEAD, D_MODEL), F32),
                        pltpu.VMEM((rows, D_MODEL), BF16)],
        compiler_params=_params(2),
        name="conv_prompt",
    )(x, wpw1, bpw1, wdw, bdw, lng, lnb, wpw2, bpw2, mg, mb)


def _conv_sample_body(layer, x_ref, st_ref, wpw1_ref, bpw1_ref, wdw_ref, bdw_ref, lng_ref, lnb_ref,
                      wpw2_ref, bpw2_ref, mg_ref, mb_ref, y_ref, nst_ref):
    del layer
    steps, bt, _ = x_ref.shape
    hist = CONV_WIDTH - 1
    x = x_ref[...].reshape(steps * bt, D_MODEL)
    z = _mm(x, wpw1_ref[...]) + bpw1_ref[...]
    u = z[:, :D_MODEL] * jax.nn.sigmoid(z[:, D_MODEL:])

    def window(j):
        if j < hist:
            return st_ref[0, j]
        return u[(j - hist) * bt:(j - hist + 1) * bt, :]

    cs = []
    for t in range(steps):
        acc = jnp.zeros((bt, D_MODEL), F32) + bdw_ref[...]
        for k in range(CONV_WIDTH):
            acc = acc + window(t + k) * wdw_ref[k:k + 1, :]
        c = _ln(acc, lng_ref[...], lnb_ref[...])
        cs.append(c * jax.nn.sigmoid(c))
    c = jnp.concatenate(cs, axis=0)
    mix = _mm(c, wpw2_ref[...]) + bpw2_ref[...]
    y = _ln(DN_ALPHA * x + mix, mg_ref[...], mb_ref[...])
    y_ref[...] = y.reshape(steps, bt, D_MODEL)
    for j in range(hist):
        nst_ref[j] = window(j + steps)


def _conv_sample(layer, x_tm, st_tm, wpw1, bpw1, wdw, bdw, lng, lnb, wpw2, bpw2, mg, mb):
    steps, bsz, _ = x_tm.shape
    hist = CONV_WIDTH - 1
    return pl.pallas_call(
        functools.partial(_conv_sample_body, layer),
        grid=(bsz // S_BT,),
        in_specs=[pl.BlockSpec((steps, S_BT, D_MODEL), lambda i: (0, i, 0)),
                  pl.BlockSpec((1, hist, S_BT, D_MODEL), lambda i: (layer, 0, i, 0)),
                  _const_spec((D_MODEL, 2 * D_MODEL)), _const_spec((1, 2 * D_MODEL)),
                  _const_spec((CONV_WIDTH, D_MODEL)), _const_spec((1, D_MODEL)),
                  _const_spec((1, D_MODEL)), _const_spec((1, D_MODEL)),
                  _const_spec((D_MODEL, D_MODEL)), _const_spec((1, D_MODEL)),
                  _const_spec((1, D_MODEL)), _const_spec((1, D_MODEL))],
        out_specs=[pl.BlockSpec((steps, S_BT, D_MODEL), lambda i: (0, i, 0)),
                   pl.BlockSpec((hist, S_BT, D_MODEL), lambda i: (0, i, 0))],
        out_shape=[jax.ShapeDtypeStruct((steps, bsz, D_MODEL), F32),
                   jax.ShapeDtypeStruct((hist, bsz, D_MODEL), F32)],
        compiler_params=_params(1),
        name="conv_sample",
    )(x_tm, st_tm, wpw1, bpw1, wdw, bdw, lng, lnb, wpw2, bpw2, mg, mb)


def _headwise_coefs(w):
    g = w.shape[0]
    rows = []
    for delta in range(-(ML_QKV_BLOCK - 1), ML_QKV_BLOCK):
        cols = []
        for o in range(ML_QKV_BLOCK):
            i = o + delta
            cols.append(w[:, i, o] if 0 <= i < ML_QKV_BLOCK else jnp.zeros((g,), w.dtype))
        rows.append(jnp.stack(cols, axis=-1).reshape(g * ML_QKV_BLOCK))
    return jnp.stack(rows, axis=0)


def _headwise(x, coef_ref):
    outs = []
    for j in range(x.shape[1] // LANES):
        cols = slice(j * LANES, (j + 1) * LANES)
        xj = x[:, cols]
        acc = xj * coef_ref[ML_QKV_BLOCK - 1:ML_QKV_BLOCK, cols]
        for delta in range(-(ML_QKV_BLOCK - 1), ML_QKV_BLOCK):
            if delta == 0:
                continue
            rolled = pltpu.roll(xj, (-delta) % LANES, axis=1)
            acc = acc + rolled * coef_ref[delta + ML_QKV_BLOCK - 1:delta + ML_QKV_BLOCK, cols]
        outs.append(acc)
    return jnp.concatenate(outs, axis=1)


def _gates(q, k, v, wif_ref, bif_ref):
    g = _mm(q, wif_ref[0:D_INNER, :])
    g = g + _mm(k, wif_ref[D_INNER:2 * D_INNER, :])
    g = g + _mm(v, wif_ref[2 * D_INNER:3 * D_INNER, :])
    return g + bif_ref[...]


def _cell_head(qh, kh, vh, igc, lfc, c_old, n_old, m_old):
    length = qh.shape[0]
    row = lax.broadcasted_iota(jnp.int32, (length, length), 0)
    col = lax.broadcasted_iota(jnp.int32, (length, length), 1)
    causal = col <= row
    eye = col == row

    def to_row(c):
        return jnp.sum(jnp.where(eye, c, 0.0), axis=0, keepdims=True)

    lfr = to_row(lfc)
    bc = jnp.sum(jnp.where(causal, lfr, 0.0), axis=1, keepdims=True)
    br = to_row(bc)
    igr = to_row(igc)
    d = jnp.where(causal, bc - br + igr, -jnp.inf)
    inter = bc + m_old
    m_t = jnp.maximum(inter, jnp.max(d, axis=1, keepdims=True))
    qb = qh.astype(BF16)
    vb = vh.astype(BF16)
    s = lax.dot_general(qb, kh.astype(BF16), (((1,), (1,)), ((), ())), preferred_element_type=F32)
    p = s * jnp.exp(d - m_t)
    a = jnp.exp(inter - m_t)
    num = a * jnp.dot(qb, c_old.astype(BF16), preferred_element_type=F32)
    num = num + jnp.dot(p.astype(BF16), vb, preferred_element_type=F32)
    den = a * jnp.sum(qh * n_old, axis=1, keepdims=True) + jnp.sum(p, axis=1, keepdims=True)
    h = num * (1.0 / jnp.maximum(jnp.abs(den), jnp.exp(-m_t)))

    b_last = bc[length - 1:length, :]
    gc = b_last - bc + igc
    m_new = jnp.maximum(b_last + m_old, jnp.max(gc, axis=0, keepdims=True))
    decay = jnp.exp(b_last + m_old - m_new)
    wk = kh * jnp.exp(gc - m_new)
    c_new = decay * c_old + lax.dot_general(wk.astype(BF16), vb, (((0,), (0,)), ((), ())),
                                            preferred_element_type=F32)
    n_new = decay * n_old + jnp.sum(wk, axis=0, keepdims=True)
    return h, c_new, n_new, m_new


def _head_out(h, o_pre, xc, ng, sk):
    mu = jnp.mean(h, axis=-1, keepdims=True)
    hc = h - mu
    var = jnp.mean(hc * hc, axis=-1, keepdims=True)
    hn = hc * lax.rsqrt(var + LN_EPS) * ng
    return jax.nn.sigmoid(o_pre) * (hn + sk * xc)


def _mlstm_prompt_body(x_ref, win_ref, cw_ref, cb_ref, cq_ref, ck_ref, cv_ref, wif_ref, bif_ref,
                       ng_ref, sk_ref, wdn_ref, mg_ref, mb_ref,
                       y_ref, mlc_ref, c_ref, n_ref, m_ref, xbuf_ref):
    t = pl.program_id(1)
    rows = ML_CHUNK
    hist = ML_CONV_WIDTH - 1

    @pl.when(t == 0)
    def _():
        xbuf_ref[0:ML_HEADROWS, :] = jnp.zeros((ML_HEADROWS, D_INNER), F32)
        c_ref[...] = jnp.zeros_like(c_ref)
        n_ref[...] = jnp.zeros_like(n_ref)
        m_ref[...] = jnp.zeros_like(m_ref)

    x = x_ref[0]
    z = _mm(x, win_ref[...])
    xm = z[:, :D_INNER]
    o_pre = z[:, D_INNER:]
    xbuf_ref[ML_HEADROWS:ML_HEADROWS + rows, :] = xm
    xc = jnp.zeros((rows, D_INNER), F32) + cb_ref[...]
    for k in range(ML_CONV_WIDTH):
        off = ML_HEADROWS - hist + k
        xc = xc + xbuf_ref[off:off + rows, :] * cw_ref[k:k + 1, :]
    xc = xc * jax.nn.sigmoid(xc)

    @pl.when(t == pl.num_programs(1) - 1)
    def _():
        mlc_ref[0] = xbuf_ref[ML_HEADROWS + rows - hist:ML_HEADROWS + rows, :]

    xbuf_ref[0:ML_HEADROWS, :] = xbuf_ref[rows:rows + ML_HEADROWS, :]

    q = _headwise(xc, cq_ref)
    k = _headwise(xc, ck_ref)
    v = _headwise(xm, cv_ref)
    gates = _gates(q, k, v, wif_ref, bif_ref)

    mix = jnp.zeros((rows, D_MODEL), F32)
    for h in range(ML_HEADS):
        cols = slice(h * ML_HEAD_DIM, (h + 1) * ML_HEAD_DIM)
        igc = gates[:, h:h + 1]
        lfc = _log_sigmoid(gates[:, ML_HEADS + h:ML_HEADS + h + 1])
        hh, c_new, n_new, m_new = _cell_head(q[:, cols], k[:, cols] * K_SCALE, v[:, cols], igc, lfc,
                                             c_ref[0, h], n_ref[0, h:h + 1, :], m_ref[0, :, h:h + 1])
        c_ref[0, h] = c_new
        n_ref[0, h:h + 1, :] = n_new
        m_ref[0, :, h:h + 1] = m_new
        out = _head_out(hh, o_pre[:, cols], xc[:, cols], ng_ref[:, cols], sk_ref[:, cols])
        mix = mix + _mm(out, wdn_ref[cols, :])
    y_ref[0] = _ln(DN_ALPHA * x + mix, mg_ref[...], mb_ref[...])


def _mlstm_prompt(x, win, cw, cb, cq, ck, cv, wif, bif, ng, sk, wdn, mg, mb):
    bsz, seq, _ = x.shape
    rows = ML_CHUNK
    hist = ML_CONV_WIDTH - 1
    ncoef = 2 * ML_QKV_BLOCK - 1
    return pl.pallas_call(
        _mlstm_prompt_body,
        grid=(bsz, seq // rows),
        in_specs=[pl.BlockSpec((1, rows, D_MODEL), lambda b, t: (b, t, 0)),
                  _const_spec((D_MODEL, 2 * D_INNER)),
                  _const_spec((ML_CONV_WIDTH, D_INNER)), _const_spec((1, D_INNER)),
                  _const_spec((ncoef, D_INNER)), _const_spec((ncoef, D_INNER)),
                  _const_spec((ncoef, D_INNER)),
                  _const_spec((3 * D_INNER, GATE_LANES)), _const_spec((1, GATE_LANES)),
                  _const_spec((1, D_INNER)), _const_spec((1, D_INNER)),
                  _const_spec((D_INNER, D_MODEL)),
                  _const_spec((1, D_MODEL)), _const_spec((1, D_MODEL))],
        out_specs=[pl.BlockSpec((1, rows, D_MODEL), lambda b, t: (b, t, 0)),
                   pl.BlockSpec((1, hist, D_INNER), lambda b, t: (b, 0, 0)),
                   pl.BlockSpec((1, ML_HEADS, ML_HEAD_DIM, ML_HEAD_DIM), lambda b, t: (b, 0, 0, 0)),
                   pl.BlockSpec((1, ML_HEADS, ML_HEAD_DIM), lambda b, t: (b, 0, 0)),
                   pl.BlockSpec((1, 1, ML_HEADS), lambda b, t: (b, 0, 0))],
        out_shape=[jax.ShapeDtypeStruct((bsz, seq, D_MODEL), F32),
                   jax.ShapeDtypeStruct((bsz, hist, D_INNER), F32),
                   jax.ShapeDtypeStruct((bsz, ML_HEADS, ML_HEAD_DIM, ML_HEAD_DIM), F32),
                   jax.ShapeDtypeStruct((bsz, ML_HEADS, ML_HEAD_DIM), F32),
                   jax.ShapeDtypeStruct((bsz, 1, ML_HEADS), F32)],
        scratch_shapes=[pltpu.VMEM((ML_HEADROWS + rows, D_INNER), F32)],
        compiler_params=_params(2),
        name="mlstm_prompt",
    )(x, win, cw, cb, cq, ck, cv, wif, bif, ng, sk, wdn, mg, mb)


def _mlstm_sample_proj_body(layer, x_ref, st_ref, win_ref, cw_ref, cb_ref, cq_ref, ck_ref, cv_ref,
                            wif_ref, bif_ref, q_ref, k_ref, v_ref, g_ref, xc_ref, op_ref, nst_ref):
    del layer
    steps, bt, _ = x_ref.shape
    hist = ML_CONV_WIDTH - 1
    x = x_ref[...].reshape(steps * bt, D_MODEL)
    z = _mm(x, win_ref[...])
    xm = z[:, :D_INNER]
    op_ref[...] = z[:, D_INNER:].reshape(steps, bt, D_INNER)

    def window(j):
        if j < hist:
            return st_ref[0, j]
        return xm[(j - hist) * bt:(j - hist + 1) * bt, :]

    xcs = []
    for t in range(steps):
        acc = jnp.zeros((bt, D_INNER), F32) + cb_ref[...]
        for k in range(ML_CONV_WIDTH):
            acc = acc + window(t + k) * cw_ref[k:k + 1, :]
        xcs.append(acc * jax.nn.sigmoid(acc))
    xc = jnp.concatenate(xcs, axis=0)
    for j in range(hist):
        nst_ref[j] = window(j + steps)
    q = _headwise(xc, cq_ref)
    k = _headwise(xc, ck_ref)
    v = _headwise(xm, cv_ref)
    gates = _gates(q, k, v, wif_ref, bif_ref)
    q_ref[...] = q.reshape(steps, bt, D_INNER)
    k_ref[...] = k.reshape(steps, bt, D_INNER)
    v_ref[...] = v.reshape(steps, bt, D_INNER)
    g_ref[...] = gates.reshape(steps, bt, GATE_LANES)
    xc_ref[...] = xc.reshape(steps, bt, D_INNER)


def _mlstm_sample_proj(layer, x_tm, st_tm, win, cw, cb, cq, ck, cv, wif, bif):
    steps, bsz, _ = x_tm.shape
    hist = ML_CONV_WIDTH - 1
    ncoef = 2 * ML_QKV_BLOCK - 1
    wide = pl.BlockSpec((steps, S_BT, D_INNER), lambda i: (0, i, 0))
    wide_shape = jax.ShapeDtypeStruct((steps, bsz, D_INNER), F32)
    return pl.pallas_call(
        functools.partial(_mlstm_sample_proj_body, layer),
        grid=(bsz // S_BT,),
        in_specs=[pl.BlockSpec((steps, S_BT, D_MODEL), lambda i: (0, i, 0)),
                  pl.BlockSpec((1, hist, S_BT, D_INNER), lambda i: (layer, 0, i, 0)),
                  _const_spec((D_MODEL, 2 * D_INNER)),
                  _const_spec((ML_CONV_WIDTH, D_INNER)), _const_spec((1, D_INNER)),
                  _const_spec((ncoef, D_INNER)), _const_spec((ncoef, D_INNER)),
                  _const_spec((ncoef, D_INNER)),
                  _const_spec((3 * D_INNER, GATE_LANES)), _const_spec((1, GATE_LANES))],
        out_specs=[wide, wide, wide,
                   pl.BlockSpec((steps, S_BT, GATE_LANES), lambda i: (0, i, 0)),
                   wide, wide,
                   pl.BlockSpec((hist, S_BT, D_INNER), lambda i: (0, i, 0))],
        out_shape=[wide_shape, wide_shape, wide_shape,
                   jax.ShapeDtypeStruct((steps, bsz, GATE_LANES), F32),
                   wide_shape, wide_shape,
                   jax.ShapeDtypeStruct((hist, bsz, D_INNER), F32)],
        compiler_params=_params(1),
        name="mlstm_sample_proj",
    )(x_tm, st_tm, win, cw, cb, cq, ck, cv, wif, bif)


def _mlstm_sample_cell_body(steps, q_ref, k_ref, v_ref, g_ref, c_ref, n_ref, m_ref,
                            h_ref, co_ref, no_ref, mo_ref):
    gates = g_ref[0]
    valid = lax.broadcasted_iota(jnp.int32, (S_TPAD, 1), 0) < steps
    for h in range(ML_HEADS):
        cols = slice(h * ML_HEAD_DIM, (h + 1) * ML_HEAD_DIM)
        igc = jnp.where(valid, gates[:, h:h + 1], -jnp.inf)
        lfc = jnp.where(valid, _log_sigmoid(gates[:, ML_HEADS + h:ML_HEADS + h + 1]), 0.0)
        hh, c_new, n_new, m_new = _cell_head(q_ref[0, :, cols], k_ref[0, :, cols] * K_SCALE,
                                             v_ref[0, :, cols], igc, lfc,
                                             c_ref[0, 0, h], n_ref[0, 0, h:h + 1, :],
                                             m_ref[0, 0, :, h:h + 1])
        h_ref[0, :, cols] = hh
        co_ref[0, h] = c_new
        no_ref[0, h:h + 1, :] = n_new
        mo_ref[0, :, h:h + 1] = m_new


def _mlstm_sample_cell(layer, steps, q_bm, k_bm, v_bm, g_bm, c_st, n_st, m_st):
    bsz = q_bm.shape[0]
    wide = pl.BlockSpec((1, S_TPAD, D_INNER), lambda b: (b, 0, 0))
    return pl.pallas_call(
        functools.partial(_mlstm_sample_cell_body, steps),
        grid=(bsz,),
        in_specs=[wide, wide, wide,
                  pl.BlockSpec((1, S_TPAD, GATE_LANES), lambda b: (b, 0, 0)),
                  pl.BlockSpec((1, 1, ML_HEADS, ML_HEAD_DIM, ML_HEAD_DIM),
                               lambda b: (layer, b, 0, 0, 0)),
                  pl.BlockSpec((1, 1, ML_HEADS, ML_HEAD_DIM), lambda b: (layer, b, 0, 0)),
                  pl.BlockSpec((1, 1, 1, ML_HEADS), lambda b: (layer, b, 0, 0))],
        out_specs=[wide,
                   pl.BlockSpec((1, ML_HEADS, ML_HEAD_DIM, ML_HEAD_DIM), lambda b: (b, 0, 0, 0)),
                   pl.BlockSpec((1, ML_HEADS, ML_HEAD_DIM), lambda b: (b, 0, 0)),
                   pl.BlockSpec((1, 1, ML_HEADS), lambda b: (b, 0, 0))],
        out_shape=[jax.ShapeDtypeStruct((bsz, S_TPAD, D_INNER), F32),
                   jax.ShapeDtypeStruct((bsz, ML_HEADS, ML_HEAD_DIM, ML_HEAD_DIM), F32),
                   jax.ShapeDtypeStruct((bsz, ML_HEADS, ML_HEAD_DIM), F32),
                   jax.ShapeDtypeStruct((bsz, 1, ML_HEADS), F32)],
        compiler_params=_params(1),
        name="mlstm_sample_cell",
    )(q_bm, k_bm, v_bm, g_bm, c_st, n_st, m_st)


def _mlstm_sample_out_body(x_ref, h_ref, xc_ref, op_ref, ng_ref, sk_ref, wdn_ref, mg_ref, mb_ref,
                           y_ref):
    steps, bt, _ = x_ref.shape
    n = steps * bt
    x = x_ref[...].reshape(n, D_MODEL)
    mix = jnp.zeros((n, D_MODEL), F32)
    for h in range(ML_HEADS):
        cols = slice(h * ML_HEAD_DIM, (h + 1) * ML_HEAD_DIM)
        out = _head_out(h_ref[:, :, cols].reshape(n, ML_HEAD_DIM),
                        op_ref[:, :, cols].reshape(n, ML_HEAD_DIM),
                        xc_ref[:, :, cols].reshape(n, ML_HEAD_DIM),
                        ng_ref[:, cols], sk_ref[:, cols])
        mix = mix + _mm(out, wdn_ref[cols, :])
    y_ref[...] = _ln(DN_ALPHA * x + mix, mg_ref[...], mb_ref[...]).reshape(steps, bt, D_MODEL)


def _mlstm_sample_out(x_tm, h_tm, xc_tm, op_tm, ng, sk, wdn, mg, mb):
    steps, bsz, _ = x_tm.shape
    wide = pl.BlockSpec((steps, S_BT, D_INNER), lambda i: (0, i, 0))
    narrow = pl.BlockSpec((steps, S_BT, D_MODEL), lambda i: (0, i, 0))
    return pl.pallas_call(
        _mlstm_sample_out_body,
        grid=(bsz // S_BT,),
        in_specs=[narrow, wide, wide, wide,
                  _const_spec((1, D_INNER)), _const_spec((1, D_INNER)),
                  _const_spec((D_INNER, D_MODEL)),
                  _const_spec((1, D_MODEL)), _const_spec((1, D_MODEL))],
        out_specs=narrow,
        out_shape=jax.ShapeDtypeStruct((steps, bsz, D_MODEL), F32),
        compiler_params=_params(1),
        name="mlstm_sample_out",
    )(x_tm, h_tm, xc_tm, op_tm, ng, sk, wdn, mg, mb)


def _row(v):
    return v.reshape(1, -1)


def kernel(x_prompt, x_sample, state_conv, state_mlstm_conv, state_mlstm_C, state_mlstm_n, state_mlstm_m, conv_w_pw1, conv_b_pw1, conv_w_dw, conv_b_dw, conv_ln_g, conv_ln_b, conv_w_pw2, conv_b_pw2, ml_w_in, ml_conv_w, ml_conv_b, ml_w_q, ml_w_k, ml_w_v, ml_w_if, ml_b_if, ml_norm_g, ml_skip, ml_w_down, ln_mix_g, ln_mix_b, ffn_w1, ffn_w2, ln_ffn_g, ln_ffn_b):
    bsz, seq, _ = x_prompt.shape
    dbs, steps, _ = x_sample.shape

    wpw1 = conv_w_pw1.astype(BF16)
    wpw2 = conv_w_pw2.astype(BF16)
    win = ml_w_in.astype(BF16)
    wdn = ml_w_down.astype(BF16)
    w1 = ffn_w1.astype(BF16)
    w2 = ffn_w2.astype(BF16)
    wif = jnp.pad(ml_w_if, ((0, 0), (0, 0), (0, GATE_LANES - 2 * ML_HEADS))).astype(BF16)
    bif = jnp.pad(ml_b_if, ((0, 0), (0, GATE_LANES - 2 * ML_HEADS)))

    xs = jnp.transpose(x_sample, (1, 0, 2))
    conv_st_tm = jnp.transpose(state_conv, (0, 2, 1, 3))
    mlc_st_tm = jnp.transpose(state_mlstm_conv, (0, 2, 1, 3))
    m_st = state_mlstm_m.reshape(state_mlstm_m.shape[0], dbs, 1, ML_HEADS)

    xp = x_prompt
    conv_p, conv_s, mlc_p, mlc_s = [], [], [], []
    c_p, c_s, n_p, n_s, m_p, m_s = [], [], [], [], [], []
    for i in range(DEPTH):
        j = i // 2
        mg, mb = _row(ln_mix_g[i]), _row(ln_mix_b[i])
        if i % 2 == 0:
            args = (wpw1[j], _row(conv_b_pw1[j]), conv_w_dw[j], _row(conv_b_dw[j]),
                    _row(conv_ln_g[j]), _row(conv_ln_b[j]), wpw2[j], _row(conv_b_pw2[j]), mg, mb)
            xp, st = _conv_prompt(xp, *args)
            conv_p.append(st)
            xs, st = _conv_sample(j, xs, conv_st_tm, *args)
            conv_s.append(jnp.transpose(st, (1, 0, 2)))
        else:
            cq, ck, cv = (_headwise_coefs(w[j]) for w in (ml_w_q, ml_w_k, ml_w_v))
            front = (win[j], ml_conv_w[j], _row(ml_conv_b[j]), cq, ck, cv, wif[j], _row(bif[j]))
            back = (_row(ml_norm_g[j]), _row(ml_skip[j]), wdn[j], mg, mb)
            xp, st, c_new, n_new, m_new = _mlstm_prompt(xp, *front, *back)
            mlc_p.append(st)
            c_p.append(c_new)
            n_p.append(n_new)
            m_p.append(m_new.reshape(bsz, ML_HEADS))

            q, k, v, g, xc, op, st = _mlstm_sample_proj(j, xs, mlc_st_tm, *front)
            mlc_s.append(jnp.transpose(st, (1, 0, 2)))

            def to_cell(a):
                a = jnp.transpose(a, (1, 0, 2))
                return jnp.pad(a, ((0, 0), (0, S_TPAD - steps), (0, 0)))

            h_bm, c_new, n_new, m_new = _mlstm_sample_cell(
                j, steps, to_cell(q), to_cell(k), to_cell(v), to_cell(g),
                state_mlstm_C, state_mlstm_n, m_st)
            c_s.append(c_new)
            n_s.append(n_new)
            m_s.append(m_new.reshape(dbs, ML_HEADS))
            h_tm = jnp.transpose(h_bm[:, :steps], (1, 0, 2))
            xs = _mlstm_sample_out(xs, h_tm, xc, op, *back)

        fg, fb = _row(ln_ffn_g[i]), _row(ln_ffn_b[i])
        xp = _ffn(xp.reshape(bsz * seq, D_MODEL), w1[i], w2[i], fg, fb).reshape(bsz, seq, D_MODEL)
        xs = _ffn(xs.reshape(steps * dbs, D_MODEL), w1[i], w2[i], fg, fb).reshape(steps, dbs, D_MODEL)

    y_sample = jnp.transpose(xs, (1, 0, 2))
    return (xp, y_sample, jnp.stack(conv_p), jnp.stack(conv_s), jnp.stack(mlc_p), jnp.stack(mlc_s),
            jnp.stack(c_p), jnp.stack(c_s), jnp.stack(n_p), jnp.stack(n_s),
            jnp.stack(m_p), jnp.stack(m_s))
```

```python
import functools

import jax
import jax.numpy as jnp
from jax import lax
from jax.experimental import pallas as pl
from jax.experimental.pallas import tpu as pltpu

D_MODEL = 1024
DEPTH = 4
CONV_WIDTH = 31
D_FF = 4 * D_MODEL
D_INNER = 2 * D_MODEL
ML_HEADS = 4
ML_HEAD_DIM = D_INNER // ML_HEADS
ML_QKV_BLOCK = 4
ML_CONV_WIDTH = 4
DN_ALPHA = (2.0 * DEPTH) ** 0.25
LN_EPS = 1e-5
K_SCALE = ML_HEAD_DIM ** -0.5

LANES = 128
SUBLANES = 8
GATE_LANES = LANES
VMEM_LIMIT = 56 * 1024 * 1024

FF_CHUNK = 1024
FF_CHUNKS = D_FF // FF_CHUNK
FFN_ROWS = 512
CONV_ROWS = 256
CONV_HEAD = 32
CONV_RCHUNK = 64
CONV_RSUB = 32
ML_CHUNK = 256
ML_HEADROWS = 8
ML_RCHUNK = 32
OGATE_CHUNKS = 8
S_BT = 32
S_TPAD = 8

BF16 = jnp.bfloat16
F32 = jnp.float32


def _const_spec(shape):
    nd = len(shape)
    return pl.BlockSpec(shape, lambda *_: (0,) * nd, pipeline_mode=pl.Buffered(1))


def _params(n_grid):
    return pltpu.CompilerParams(dimension_semantics=("arbitrary",) * n_grid,
                                vmem_limit_bytes=VMEM_LIMIT)


def _ln(x, g, b):
    mu = jnp.mean(x, axis=-1, keepdims=True)
    xc = x - mu
    var = jnp.mean(xc * xc, axis=-1, keepdims=True)
    return xc * lax.rsqrt(var + LN_EPS) * g + b


def _mm(a, w):
    return jnp.dot(a.astype(BF16), w, preferred_element_type=F32)


def _log_sigmoid(x):
    return jnp.minimum(x, 0.0) - jnp.log1p(jnp.exp(-jnp.abs(x)))


def _ffn_body(x_ref, w1_ref, w2_ref, g_ref, b_ref, o_ref):
    x = x_ref[...]
    xb = x.astype(BF16)
    acc = jnp.zeros_like(x)
    for j in range(FF_CHUNKS):
        h = jnp.dot(xb, w1_ref[j], preferred_element_type=F32)
        h = jnp.square(jnp.maximum(h, 0.0))
        acc = acc + jnp.dot(h.astype(BF16), w2_ref[j], preferred_element_type=F32)
    o_ref[...] = _ln(DN_ALPHA * x + acc, g_ref[...], b_ref[...])


def _ffn(x2, w1, w2, g, b):
    n = x2.shape[0]
    rows = min(FFN_ROWS, n)
    return pl.pallas_call(
        _ffn_body,
        grid=(n // rows,),
        in_specs=[pl.BlockSpec((rows, D_MODEL), lambda i: (i, 0)),
                  _const_spec((FF_CHUNKS, D_MODEL, FF_CHUNK)),
                  _const_spec((FF_CHUNKS, FF_CHUNK, D_MODEL)),
                  _const_spec((1, D_MODEL)), _const_spec((1, D_MODEL))],
        out_specs=pl.BlockSpec((rows, D_MODEL), lambda i: (i, 0)),
        out_shape=jax.ShapeDtypeStruct((n, D_MODEL), F32),
        compiler_params=_params(1),
        name="ffn",
    )(x2, w1, w2, g, b)


def _dwconv_taps_by_phase():
    first = CONV_HEAD - (CONV_WIDTH - 1)
    groups = []
    for r in range(SUBLANES):
        taps = [(k, (first + k) // SUBLANES) for k in range(CONV_WIDTH)
                if (first + k) % SUBLANES == r]
        groups.append((r, taps))
    return groups


def _conv_ffn_prompt_body(tiles_per_seq, x_ref, wpw1_ref, bpw1_ref, wdw_ref, bdw_ref, lng_ref,
                          lnb_ref, wpw2_ref, bpw2_ref, mg_ref, mb_ref, w1_ref, w2_ref, fg_ref,
                          fb_ref, y_ref, st_ref, ush_ref, cf_ref, c_ref, mid_ref, midb_ref,
                          facc_ref):
    s = pl.program_id(0)
    n_tiles = pl.num_programs(0) - 1
    t = s % tiles_per_seq
    rows = CONV_ROWS
    span = rows + CONV_HEAD

    @pl.when(s == 0)
    def _():
        mid_ref[...] = jnp.zeros(mid_ref.shape, F32)
        midb_ref[...] = jnp.zeros(midb_ref.shape, BF16)

    @pl.when(t == 0)
    def _():
        ush_ref[0, 0:CONV_HEAD, :] = jnp.zeros((CONV_HEAD, D_MODEL), F32)

    def mlp_chunk(c):
        h = jnp.dot(midb_ref[...], w1_ref[c], preferred_element_type=F32)
        h = jnp.square(jnp.maximum(h, 0.0))
        return jnp.dot(h.astype(BF16), w2_ref[c], preferred_element_type=F32)

    x = x_ref[...]
    z = _mm(x, wpw1_ref[...]) + bpw1_ref[...]
    u = z[:, :D_MODEL] * jax.nn.sigmoid(z[:, D_MODEL:])
    ush_ref[0, CONV_HEAD:span, :] = u
    for r in range(1, SUBLANES):
        ush_ref[r, 0:span - SUBLANES, :] = ush_ref[0, r:r + span - SUBLANES, :]

    vregs = CONV_RSUB // SUBLANES
    groups = _dwconv_taps_by_phase()

    facc_ref[...] = jnp.zeros(facc_ref.shape, F32)

    def chunk(i, carry):
        facc_ref[...] += mlp_chunk(i)
        r0 = pl.multiple_of(i * CONV_RCHUNK, CONV_RCHUNK)
        for sub in range(CONV_RCHUNK // CONV_RSUB):
            rs = r0 + sub * CONV_RSUB
            for j in range(D_MODEL // LANES):
                cols = slice(j * LANES, (j + 1) * LANES)
                acc = None
                for r, taps in groups:
                    lo = min(a for _, a in taps)
                    hi = max(a for _, a in taps)
                    big = ush_ref[r, pl.ds(rs + lo * SUBLANES, CONV_RSUB + (hi - lo) * SUBLANES), cols]
                    for k, a in taps:
                        win = big[(a - lo) * SUBLANES:(a - lo) * SUBLANES + CONV_RSUB, :]
                        term = win.reshape(vregs, SUBLANES, LANES) * wdw_ref[k, :, cols][None]
                        acc = term if acc is None else acc + term
                cf_ref[pl.ds(rs, CONV_RSUB), cols] = acc.reshape(CONV_RSUB, LANES)
        c = _ln(cf_ref[pl.ds(r0, CONV_RCHUNK), :] + bdw_ref[...], lng_ref[...], lnb_ref[...])
        c = c * jax.nn.sigmoid(c)
        c_ref[pl.ds(r0, CONV_RCHUNK), :] = c.astype(BF16)
        return carry

    assert rows // CONV_RCHUNK == FF_CHUNKS
    lax.fori_loop(0, FF_CHUNKS, chunk, 0)

    y_ref[...] = _ln(DN_ALPHA * mid_ref[...] + facc_ref[...], fg_ref[...], fb_ref[...])
    mix = jnp.dot(c_ref[...], wpw2_ref[...], preferred_element_type=F32) + bpw2_ref[...]
    mid = _ln(DN_ALPHA * x + mix, mg_ref[...], mb_ref[...])
    mid_ref[...] = mid
    midb_ref[...] = mid.astype(BF16)

    @pl.when(jnp.logical_and(t == tiles_per_seq - 1, s < n_tiles))
    def _():
        st_ref[0] = ush_ref[0, span - (CONV_WIDTH - 1):span, :]

    ush_ref[0, 0:CONV_HEAD, :] = ush_ref[0, rows:span, :]


def _conv_ffn_prompt(x, wpw1, bpw1, wdw, bdw, lng, lnb, wpw2, bpw2, mg, mb, w1, w2, fg, fb):
    bsz, seq, _ = x.shape
    rows = CONV_ROWS
    tiles_per_seq = seq // rows
    n_tiles = bsz * tiles_per_seq

    def tile_in(s):
        return (jnp.minimum(s, n_tiles - 1), 0)

    def tile_out(s):
        return (jnp.maximum(s - 1, 0), 0)

    y, st = pl.pallas_call(
        functools.partial(_conv_ffn_prompt_body, tiles_per_seq),
        grid=(n_tiles + 1,),
        in_specs=[pl.BlockSpec((rows, D_MODEL), tile_in),
                  _const_spec((D_MODEL, 2 * D_MODEL)), _const_spec((1, 2 * D_MODEL)),
                  _const_spec((CONV_WIDTH, SUBLANES, D_MODEL)), _const_spec((1, D_MODEL)),
                  _const_spec((1, D_MODEL)), _const_spec((1, D_MODEL)),
                  _const_spec((D_MODEL, D_MODEL)), _const_spec((1, D_MODEL)),
                  _const_spec((1, D_MODEL)), _const_spec((1, D_MODEL)),
                  _const_spec((FF_CHUNKS, D_MODEL, FF_CHUNK)),
                  _const_spec((FF_CHUNKS, FF_CHUNK, D_MODEL)),
                  _const_spec((1, D_MODEL)), _const_spec((1, D_MODEL))],
        out_specs=[pl.BlockSpec((rows, D_MODEL), tile_out),
                   pl.BlockSpec((1, CONV_WIDTH - 1, D_MODEL),
                                lambda s: (jnp.minimum(s, n_tiles - 1) // tiles_per_seq, 0, 0))],
        out_shape=[jax.ShapeDtypeStruct((bsz * seq, D_MODEL), F32),
                   jax.ShapeDtypeStruct((bsz, CONV_WIDTH - 1, D_MODEL), F32)],
        scratch_shapes=[pltpu.VMEM((SUBLANES, rows + CONV_HEAD, D_MODEL), F32),
                        pltpu.VMEM((rows, D_MODEL), F32),
                        pltpu.VMEM((rows, D_MODEL), BF16),
                        pltpu.VMEM((rows, D_MODEL), F32),
                        pltpu.VMEM((rows, D_MODEL), BF16),
                        pltpu.VMEM((rows, D_MODEL), F32)],
        compiler_params=_params(1),
        name="conv_ffn_prompt",
    )(x.reshape(bsz * seq, D_MODEL), wpw1, bpw1, wdw, bdw, lng, lnb, wpw2, bpw2, mg, mb,
      w1, w2, fg, fb)
    return y.reshape(bsz, seq, D_MODEL), st


def _conv_sample_body(layer, x_ref, st_ref, wpw1_ref, bpw1_ref, wdw_ref, bdw_ref, lng_ref, lnb_ref,
                      wpw2_ref, bpw2_ref, mg_ref, mb_ref, y_ref, nst_ref):
    del layer
    steps, bt, _ = x_ref.shape
    hist = CONV_WIDTH - 1
    x = x_ref[...].reshape(steps * bt, D_MODEL)
    z = _mm(x, wpw1_ref[...]) + bpw1_ref[...]
    u = z[:, :D_MODEL] * jax.nn.sigmoid(z[:, D_MODEL:])

    def window(j):
        if j < hist:
            return st_ref[0, j]
        return u[(j - hist) * bt:(j - hist + 1) * bt, :]

    cs = []
    for t in range(steps):
        acc = jnp.zeros((bt, D_MODEL), F32) + bdw_ref[...]
        for k in range(CONV_WIDTH):
            acc = acc + window(t + k) * wdw_ref[k]
        c = _ln(acc, lng_ref[...], lnb_ref[...])
        cs.append(c * jax.nn.sigmoid(c))
    c = jnp.concatenate(cs, axis=0)
    mix = _mm(c, wpw2_ref[...]) + bpw2_ref[...]
    y = _ln(DN_ALPHA * x + mix, mg_ref[...], mb_ref[...])
    y_ref[...] = y.reshape(steps, bt, D_MODEL)
    for j in range(hist):
        nst_ref[j] = window(j + steps)


def _conv_sample(layer, x_tm, st_tm, wpw1, bpw1, wdw, bdw, lng, lnb, wpw2, bpw2, mg, mb):
    steps, bsz, _ = x_tm.shape
    hist = CONV_WIDTH - 1
    return pl.pallas_call(
        functools.partial(_conv_sample_body, layer),
        grid=(bsz // S_BT,),
        in_specs=[pl.BlockSpec((steps, S_BT, D_MODEL), lambda i: (0, i, 0)),
                  pl.BlockSpec((1, hist, S_BT, D_MODEL), lambda i: (layer, 0, i, 0)),
                  _const_spec((D_MODEL, 2 * D_MODEL)), _const_spec((1, 2 * D_MODEL)),
                  _const_spec((CONV_WIDTH, 1, D_MODEL)), _const_spec((1, D_MODEL)),
                  _const_spec((1, D_MODEL)), _const_spec((1, D_MODEL)),
                  _const_spec((D_MODEL, D_MODEL)), _const_spec((1, D_MODEL)),
                  _const_spec((1, D_MODEL)), _const_spec((1, D_MODEL))],
        out_specs=[pl.BlockSpec((steps, S_BT, D_MODEL), lambda i: (0, i, 0)),
                   pl.BlockSpec((hist, S_BT, D_MODEL), lambda i: (0, i, 0))],
        out_shape=[jax.ShapeDtypeStruct((steps, bsz, D_MODEL), F32),
                   jax.ShapeDtypeStruct((hist, bsz, D_MODEL), F32)],
        compiler_params=_params(1),
        name="conv_sample",
    )(x_tm, st_tm, wpw1, bpw1, wdw, bdw, lng, lnb, wpw2, bpw2, mg, mb)


def _headwise_blocks(w):
    per = LANES // ML_QKV_BLOCK
    w4 = w.reshape(-1, per, ML_QKV_BLOCK, ML_QKV_BLOCK)
    eye = jnp.eye(per, dtype=w.dtype)
    return jnp.einsum('jaio,ab->jaibo', w4, eye).reshape(-1, LANES, LANES)


def _headwise_qkv(xc, xm, wqk_ref, wv_ref, j):
    cols = slice(j * LANES, (j + 1) * LANES)
    qk = jnp.dot(xc[:, cols], wqk_ref[j], preferred_element_type=F32)
    v = jnp.dot(xm[:, cols], wv_ref[j], preferred_element_type=F32)
    return qk[:, :LANES], qk[:, LANES:], v


def _gates(q, k, v, wif_ref, bif_ref):
    g = _mm(q, wif_ref[0:D_INNER, :])
    g = g + _mm(k, wif_ref[D_INNER:2 * D_INNER, :])
    g = g + _mm(v, wif_ref[2 * D_INNER:3 * D_INNER, :])
    return g + bif_ref[...]


def _cell_head(qh, qb, kh, vb, igc, lfc, c_old, n_old, m_old):
    length = qh.shape[0]
    row = lax.broadcasted_iota(jnp.int32, (length, length), 0)
    col = lax.broadcasted_iota(jnp.int32, (length, length), 1)
    causal = col <= row
    eye = col == row

    def to_row(c):
        return jnp.sum(jnp.where(eye, c, 0.0), axis=0, keepdims=True)

    lfr = to_row(lfc)
    bc = jnp.sum(jnp.where(causal, lfr, 0.0), axis=1, keepdims=True)
    br = to_row(bc)
    igr = to_row(igc)
    d = jnp.where(causal, bc - br + igr, -jnp.inf)
    inter = bc + m_old
    m_t = jnp.maximum(inter, jnp.max(d, axis=1, keepdims=True))
    s = lax.dot_general(qb, kh.astype(BF16), (((1,), (1,)), ((), ())), preferred_element_type=F32)
    p = s * jnp.exp(d - m_t)
    a = jnp.exp(inter - m_t)
    num = a * jnp.dot(qb, c_old.astype(BF16), preferred_element_type=F32)
    num = num + jnp.dot(p.astype(BF16), vb, preferred_element_type=F32)
    den = a * jnp.sum(qh * n_old, axis=1, keepdims=True) + jnp.sum(p, axis=1, keepdims=True)
    h = num * (1.0 / jnp.maximum(jnp.abs(den), jnp.exp(-m_t)))

    b_last = bc[length - 1:length, :]
    gc = b_last - bc + igc
    m_new = jnp.maximum(b_last + m_old, jnp.max(gc, axis=0, keepdims=True))
    decay = jnp.exp(b_last + m_old - m_new)
    wk = kh * jnp.exp(gc - m_new)
    c_new = decay * c_old + lax.dot_general(wk.astype(BF16), vb, (((0,), (0,)), ((), ())),
                                            preferred_element_type=F32)
    n_new = decay * n_old + jnp.sum(wk, axis=0, keepdims=True)
    return h, c_new, n_new, m_new


def _head_out(h, o_pre, xc, ng, sk):
    mu = jnp.mean(h, axis=-1, keepdims=True)
    hc = h - mu
    var = jnp.mean(hc * hc, axis=-1, keepdims=True)
    hn = hc * lax.rsqrt(var + LN_EPS) * ng
    return jax.nn.sigmoid(o_pre) * (hn + sk * xc)


def _mlstm_prompt_body(n_prev, x_ref, winm_ref, wino_ref, cw_ref, cb_ref, wqk_ref, wv_ref, wif_ref,
                       bif_ref, ng_ref, sk_ref, wdn_ref, mg_ref, mb_ref, *rest):
    (y_ref, mlc_ref, c_all_ref, n_all_ref, m_all_ref,
     xbuf_ref, op_ref, xc_ref, q_ref, k_ref, xb_ref, xcb_ref, xmb_ref, qb_ref, kb_ref,
     vb_ref) = rest[n_prev:]
    c_ref, n_ref, m_ref = c_all_ref.at[0], n_all_ref.at[0], m_all_ref.at[0]
    t = pl.program_id(1)
    rows = ML_CHUNK
    hist = ML_CONV_WIDTH - 1

    @pl.when(t == 0)
    def _():
        xbuf_ref[0:ML_HEADROWS, :] = jnp.zeros((ML_HEADROWS, D_INNER), F32)
        c_ref[...] = jnp.zeros(c_ref.shape, F32)
        n_ref[...] = jnp.zeros(n_ref.shape, F32)
        m_ref[...] = jnp.zeros(m_ref.shape, F32)

    x = x_ref[0]
    xb_ref[...] = x.astype(BF16)
    xbuf_ref[ML_HEADROWS:ML_HEADROWS + rows, :] = jnp.dot(xb_ref[...], winm_ref[...],
                                                          preferred_element_type=F32)

    for c in range(OGATE_CHUNKS):
        op_ref[c] = jnp.dot(xb_ref[...], wino_ref[c], preferred_element_type=F32)

    for j in range(D_INNER // LANES):
        cols = slice(j * LANES, (j + 1) * LANES)
        for i in range(rows // ML_RCHUNK):
            r0 = i * ML_RCHUNK
            out_rows = slice(r0, r0 + ML_RCHUNK)
            win = xbuf_ref[r0:r0 + ML_RCHUNK + ML_HEADROWS, cols]
            xm = win[ML_HEADROWS:, :]
            acc = xm * cw_ref[hist, :, cols] + cb_ref[:, cols]
            for k in range(hist):
                off = ML_HEADROWS - hist + k
                acc = acc + win[off:off + ML_RCHUNK, :] * cw_ref[k, :, cols]
            xc = acc * jax.nn.sigmoid(acc)
            xc_ref[out_rows, cols] = xc
            xcb_ref[out_rows, cols] = xc.astype(BF16)
            xmb_ref[out_rows, cols] = xm.astype(BF16)
        q, k, v = _headwise_qkv(xcb_ref, xmb_ref, wqk_ref, wv_ref, j)
        q_ref[:, cols] = q
        k_ref[:, cols] = k
        qb_ref[:, cols] = q.astype(BF16)
        kb_ref[:, cols] = k.astype(BF16)
        vb_ref[:, cols] = v.astype(BF16)

    @pl.when(t == pl.num_programs(1) - 1)
    def _():
        mlc_ref[0, 0] = xbuf_ref[ML_HEADROWS + rows - hist:ML_HEADROWS + rows, :]

    xbuf_ref[0:ML_HEADROWS, :] = xbuf_ref[rows:rows + ML_HEADROWS, :]

    gates = jnp.dot(qb_ref[...], wif_ref[0:D_INNER, :], preferred_element_type=F32)
    gates = gates + jnp.dot(kb_ref[...], wif_ref[D_INNER:2 * D_INNER, :], preferred_element_type=F32)
    gates = gates + jnp.dot(vb_ref[...], wif_ref[2 * D_INNER:, :], preferred_element_type=F32)
    gates = gates + bif_ref[...]

    mix = jnp.zeros((rows, D_MODEL), F32)
    for h in range(ML_HEADS):
        cols = slice(h * ML_HEAD_DIM, (h + 1) * ML_HEAD_DIM)
        igc = gates[:, h:h + 1]
        lfc = _log_sigmoid(gates[:, ML_HEADS + h:ML_HEADS + h + 1])
        hh, c_new, n_new, m_new = _cell_head(q_ref[:, cols], qb_ref[:, cols],
                                             k_ref[:, cols] * K_SCALE, vb_ref[:, cols], igc, lfc,
                                             c_ref[0, h], n_ref[0, h:h + 1, :], m_ref[0, :, h:h + 1])
        c_ref[0, h] = c_new
        n_ref[0, h:h + 1, :] = n_new
        m_ref[0, :, h:h + 1] = m_new
        per_head = OGATE_CHUNKS // ML_HEADS
        o_pre = jnp.concatenate([op_ref[h * per_head + c] for c in range(per_head)], axis=1)
        out = _head_out(hh, o_pre, xc_ref[:, cols], ng_ref[:, cols], sk_ref[:, cols])
        mix = mix + _mm(out, wdn_ref[cols, :])
    y_ref[0] = _ln(DN_ALPHA * x + mix, mg_ref[...], mb_ref[...])


def _stacked_state(layer, n_layers, prev, shapes, n_inputs, first_out):
    def spec(shape):
        nd = len(shape)
        return pl.BlockSpec((1, 1) + shape[1:], lambda b, *_: (layer, b) + (0,) * (nd - 1))

    out_specs = [spec(s) for s in shapes]
    out_shapes = [jax.ShapeDtypeStruct((n_layers,) + s, F32) for s in shapes]
    prev_specs = [pl.BlockSpec(memory_space=pl.ANY) for _ in prev]
    aliases = {n_inputs + i: first_out + i for i in range(len(prev))}
    return out_specs, out_shapes, prev_specs, aliases


def _mlstm_prompt(layer, n_layers, prev, x, winm, wino, cw, cb, wqk, wv, wif, bif, ng, sk, wdn, mg,
                  mb):
    bsz, seq, _ = x.shape
    rows = ML_CHUNK
    hist = ML_CONV_WIDTH - 1
    tiles = D_INNER // LANES
    weights = (winm, wino, cw, cb, wqk, wv, wif, bif, ng, sk, wdn, mg, mb)
    st_specs, st_shapes, prev_specs, aliases = _stacked_state(
        layer, n_layers, prev,
        [(bsz, hist, D_INNER), (bsz, ML_HEADS, ML_HEAD_DIM, ML_HEAD_DIM),
         (bsz, ML_HEADS, ML_HEAD_DIM), (bsz, 1, ML_HEADS)],
        n_inputs=1 + len(weights), first_out=1)
    return pl.pallas_call(
        functools.partial(_mlstm_prompt_body, len(prev)),
        grid=(bsz, seq // rows),
        in_specs=[pl.BlockSpec((1, rows, D_MODEL), lambda b, t: (b, t, 0)),
                  _const_spec((D_MODEL, D_INNER)),
                  _const_spec((OGATE_CHUNKS, D_MODEL, D_INNER // OGATE_CHUNKS)),
                  _const_spec((ML_CONV_WIDTH, 1, D_INNER)), _const_spec((1, D_INNER)),
                  _const_spec((tiles, LANES, 2 * LANES)), _const_spec((tiles, LANES, LANES)),
                  _const_spec((3 * D_INNER, GATE_LANES)), _const_spec((1, GATE_LANES)),
                  _const_spec((1, D_INNER)), _const_spec((1, D_INNER)),
                  _const_spec((D_INNER, D_MODEL)),
                  _const_spec((1, D_MODEL)), _const_spec((1, D_MODEL))] + prev_specs,
        out_specs=[pl.BlockSpec((1, rows, D_MODEL), lambda b, t: (b, t, 0))] + st_specs,
        out_shape=[jax.ShapeDtypeStruct((bsz, seq, D_MODEL), F32)] + st_shapes,
        input_output_aliases=aliases,
        scratch_shapes=[pltpu.VMEM((ML_HEADROWS + rows, D_INNER), F32),
                        pltpu.VMEM((OGATE_CHUNKS, rows, D_INNER // OGATE_CHUNKS), F32),
                        pltpu.VMEM((rows, D_INNER), F32),
                        pltpu.VMEM((rows, D_INNER), F32),
                        pltpu.VMEM((rows, D_INNER), F32),
                        pltpu.VMEM((rows, D_MODEL), BF16),
                        pltpu.VMEM((rows, D_INNER), BF16),
                        pltpu.VMEM((rows, D_INNER), BF16),
                        pltpu.VMEM((rows, D_INNER), BF16),
                        pltpu.VMEM((rows, D_INNER), BF16),
                        pltpu.VMEM((rows, D_INNER), BF16)],
        compiler_params=_params(2),
        name="mlstm_prompt",
    )(x, *weights, *prev)


def _mlstm_sample_proj_body(layer, x_ref, st_ref, winm_ref, wino_ref, cw_ref, cb_ref, wqk_ref, wv_ref,
                            wif_ref, bif_ref, q_ref, k_ref, v_ref, g_ref, xc_ref, op_ref, nst_ref):
    del layer
    steps, bt, _ = x_ref.shape
    hist = ML_CONV_WIDTH - 1
    x = x_ref[...].reshape(steps * bt, D_MODEL)
    xb = x.astype(BF16)
    xm = jnp.dot(xb, winm_ref[...], preferred_element_type=F32)
    o_pre = jnp.concatenate([jnp.dot(xb, wino_ref[c], preferred_element_type=F32)
                             for c in range(OGATE_CHUNKS)], axis=1)
    op_ref[...] = o_pre.reshape(steps, bt, D_INNER)

    def window(j):
        if j < hist:
            return st_ref[0, j]
        return xm[(j - hist) * bt:(j - hist + 1) * bt, :]

    xcs = []
    for t in range(steps):
        acc = jnp.zeros((bt, D_INNER), F32) + cb_ref[...]
        for k in range(ML_CONV_WIDTH):
            acc = acc + window(t + k) * cw_ref[k]
        xcs.append(acc * jax.nn.sigmoid(acc))
    xc = jnp.concatenate(xcs, axis=0)
    for j in range(hist):
        nst_ref[j] = window(j + steps)
    xcb = xc.astype(BF16)
    xmb = xm.astype(BF16)
    tiles = [_headwise_qkv(xcb, xmb, wqk_ref, wv_ref, j) for j in range(D_INNER // LANES)]
    q, k, v = (jnp.concatenate(part, axis=1) for part in zip(*tiles))
    gates = _gates(q, k, v, wif_ref, bif_ref)
    q_ref[...] = q.reshape(steps, bt, D_INNER)
    k_ref[...] = k.reshape(steps, bt, D_INNER)
    v_ref[...] = v.reshape(steps, bt, D_INNER)
    g_ref[...] = gates.reshape(steps, bt, GATE_LANES)
    xc_ref[...] = xc.reshape(steps, bt, D_INNER)


def _mlstm_sample_proj(layer, x_tm, st_tm, winm, wino, cw, cb, wqk, wv, wif, bif):
    steps, bsz, _ = x_tm.shape
    hist = ML_CONV_WIDTH - 1
    tiles = D_INNER // LANES
    wide = pl.BlockSpec((steps, S_BT, D_INNER), lambda i: (0, i, 0))
    wide_shape = jax.ShapeDtypeStruct((steps, bsz, D_INNER), F32)
    return pl.pallas_call(
        functools.partial(_mlstm_sample_proj_body, layer),
        grid=(bsz // S_BT,),
        in_specs=[pl.BlockSpec((steps, S_BT, D_MODEL), lambda i: (0, i, 0)),
                  pl.BlockSpec((1, hist, S_BT, D_INNER), lambda i: (layer, 0, i, 0)),
                  _const_spec((D_MODEL, D_INNER)),
                  _const_spec((OGATE_CHUNKS, D_MODEL, D_INNER // OGATE_CHUNKS)),
                  _const_spec((ML_CONV_WIDTH, 1, D_INNER)), _const_spec((1, D_INNER)),
                  _const_spec((tiles, LANES, 2 * LANES)), _const_spec((tiles, LANES, LANES)),
                  _const_spec((3 * D_INNER, GATE_LANES)), _const_spec((1, GATE_LANES))],
        out_specs=[wide, wide, wide,
                   pl.BlockSpec((steps, S_BT, GATE_LANES), lambda i: (0, i, 0)),
                   wide, wide,
                   pl.BlockSpec((hist, S_BT, D_INNER), lambda i: (0, i, 0))],
        out_shape=[wide_shape, wide_shape, wide_shape,
                   jax.ShapeDtypeStruct((steps, bsz, GATE_LANES), F32),
                   wide_shape, wide_shape,
                   jax.ShapeDtypeStruct((hist, bsz, D_INNER), F32)],
        compiler_params=_params(1),
        name="mlstm_sample_proj",
    )(x_tm, st_tm, winm, wino, cw, cb, wqk, wv, wif, bif)


def _mlstm_sample_cell_body(steps, n_prev, q_ref, k_ref, v_ref, g_ref, c_ref, n_ref, m_ref, *rest):
    h_ref, co_ref, no_ref, mo_ref = rest[n_prev:]
    lane = pl.program_id(0) % SUBLANES
    sub = lax.broadcasted_iota(jnp.int32, (SUBLANES, 1), 0)
    mine = sub == lane

    def gather(ref, cols):
        out = jnp.zeros((S_TPAD, cols.stop - cols.start), F32)
        for t in range(steps):
            row = jnp.sum(jnp.where(mine, ref[t, :, cols], 0.0), axis=0, keepdims=True)
            out = jnp.where(sub == t, row, out)
        return out

    @pl.when(lane == 0)
    def _():
        h_ref[...] = jnp.zeros(h_ref.shape, F32)

    gates = gather(g_ref, slice(0, GATE_LANES))
    valid = sub < steps
    for h in range(ML_HEADS):
        cols = slice(h * ML_HEAD_DIM, (h + 1) * ML_HEAD_DIM)
        igc = jnp.where(valid, gates[:, h:h + 1], -jnp.inf)
        lfc = jnp.where(valid, _log_sigmoid(gates[:, ML_HEADS + h:ML_HEADS + h + 1]), 0.0)
        qh = gather(q_ref, cols)
        hh, c_new, n_new, m_new = _cell_head(qh, qh.astype(BF16), gather(k_ref, cols) * K_SCALE,
                                             gather(v_ref, cols).astype(BF16), igc, lfc,
                                             c_ref[0, 0, h], n_ref[0, 0, h:h + 1, :],
                                             m_ref[0, 0, :, h:h + 1])
        for t in range(steps):
            h_ref[t, :, cols] = jnp.where(mine, hh[t:t + 1, :], h_ref[t, :, cols])
        co_ref[0, 0, h] = c_new
        no_ref[0, 0, h:h + 1, :] = n_new
        mo_ref[0, 0, :, h:h + 1] = m_new


def _mlstm_sample_cell(layer, n_layers, prev, q_tm, k_tm, v_tm, g_tm, c_st, n_st, m_st):
    steps, bsz, _ = q_tm.shape
    wide = pl.BlockSpec((steps, SUBLANES, D_INNER), lambda b: (0, b // SUBLANES, 0))
    st_specs, st_shapes, prev_specs, aliases = _stacked_state(
        layer, n_layers, prev,
        [(bsz, ML_HEADS, ML_HEAD_DIM, ML_HEAD_DIM), (bsz, ML_HEADS, ML_HEAD_DIM),
         (bsz, 1, ML_HEADS)],
        n_inputs=7, first_out=1)
    return pl.pallas_call(
        functools.partial(_mlstm_sample_cell_body, steps, len(prev)),
        grid=(bsz,),
        in_specs=[wide, wide, wide,
                  pl.BlockSpec((steps, SUBLANES, GATE_LANES), lambda b: (0, b // SUBLANES, 0)),
                  pl.BlockSpec((1, 1, ML_HEADS, ML_HEAD_DIM, ML_HEAD_DIM),
                               lambda b: (layer, b, 0, 0, 0)),
                  pl.BlockSpec((1, 1, ML_HEADS, ML_HEAD_DIM), lambda b: (layer, b, 0, 0)),
                  pl.BlockSpec((1, 1, 1, ML_HEADS), lambda b: (layer, b, 0, 0))] + prev_specs,
        out_specs=[wide] + st_specs,
        out_shape=[jax.ShapeDtypeStruct((steps, bsz, D_INNER), F32)] + st_shapes,
        input_output_aliases=aliases,
        compiler_params=_params(1),
        name="mlstm_sample_cell",
    )(q_tm, k_tm, v_tm, g_tm, c_st, n_st, m_st, *prev)


def _mlstm_sample_out_body(x_ref, h_ref, xc_ref, op_ref, ng_ref, sk_ref, wdn_ref, mg_ref, mb_ref,
                           y_ref):
    steps, bt, _ = x_ref.shape
    n = steps * bt
    x = x_ref[...].reshape(n, D_MODEL)
    mix = jnp.zeros((n, D_MODEL), F32)
    for h in range(ML_HEADS):
        cols = slice(h * ML_HEAD_DIM, (h + 1) * ML_HEAD_DIM)
        out = _head_out(h_ref[:, :, cols].reshape(n, ML_HEAD_DIM),
                        op_ref[:, :, cols].reshape(n, ML_HEAD_DIM),
                        xc_ref[:, :, cols].reshape(n, ML_HEAD_DIM),
                        ng_ref[:, cols], sk_ref[:, cols])
        mix = mix + _mm(out, wdn_ref[cols, :])
    y_ref[...] = _ln(DN_ALPHA * x + mix, mg_ref[...], mb_ref[...]).reshape(steps, bt, D_MODEL)


def _mlstm_sample_out(x_tm, h_tm, xc_tm, op_tm, ng, sk, wdn, mg, mb):
    steps, bsz, _ = x_tm.shape
    wide = pl.BlockSpec((steps, S_BT, D_INNER), lambda i: (0, i, 0))
    narrow = pl.BlockSpec((steps, S_BT, D_MODEL), lambda i: (0, i, 0))
    return pl.pallas_call(
        _mlstm_sample_out_body,
        grid=(bsz // S_BT,),
        in_specs=[narrow, wide, wide, wide,
                  _const_spec((1, D_INNER)), _const_spec((1, D_INNER)),
                  _const_spec((D_INNER, D_MODEL)),
                  _const_spec((1, D_MODEL)), _const_spec((1, D_MODEL))],
        out_specs=narrow,
        out_shape=jax.ShapeDtypeStruct((steps, bsz, D_MODEL), F32),
        compiler_params=_params(1),
        name="mlstm_sample_out",
    )(x_tm, h_tm, xc_tm, op_tm, ng, sk, wdn, mg, mb)


def _row(v):
    return v.reshape(1, -1)


def kernel(x_prompt, x_sample, state_conv, state_mlstm_conv, state_mlstm_C, state_mlstm_n, state_mlstm_m, conv_w_pw1, conv_b_pw1, conv_w_dw, conv_b_dw, conv_ln_g, conv_ln_b, conv_w_pw2, conv_b_pw2, ml_w_in, ml_conv_w, ml_conv_b, ml_w_q, ml_w_k, ml_w_v, ml_w_if, ml_b_if, ml_norm_g, ml_skip, ml_w_down, ln_mix_g, ln_mix_b, ffn_w1, ffn_w2, ln_ffn_g, ln_ffn_b):
    bsz, seq, _ = x_prompt.shape
    dbs, steps, _ = x_sample.shape

    def per_layer(w):
        return [w[layer].astype(BF16) for layer in range(w.shape[0])]

    wpw1, wpw2, wdn = (per_layer(w) for w in (conv_w_pw1, conv_w_pw2, ml_w_down))
    n_ml = DEPTH // 2
    winm = per_layer(ml_w_in[:, :, :D_INNER])
    wino = per_layer(ml_w_in[:, :, D_INNER:]
                     .reshape(n_ml, D_MODEL, OGATE_CHUNKS, D_INNER // OGATE_CHUNKS)
                     .transpose(0, 2, 1, 3))
    w1 = per_layer(ffn_w1.reshape(DEPTH, D_MODEL, FF_CHUNKS, FF_CHUNK).transpose(0, 2, 1, 3))
    w2 = per_layer(ffn_w2.reshape(DEPTH, FF_CHUNKS, FF_CHUNK, D_MODEL))
    wif = per_layer(jnp.pad(ml_w_if, ((0, 0), (0, 0), (0, GATE_LANES - 2 * ML_HEADS))))
    bif = jnp.pad(ml_b_if, ((0, 0), (0, GATE_LANES - 2 * ML_HEADS)))

    xs = jnp.transpose(x_sample, (1, 0, 2))
    conv_st_tm = jnp.transpose(state_conv, (0, 2, 1, 3))
    mlc_st_tm = jnp.transpose(state_mlstm_conv, (0, 2, 1, 3))
    m_st = state_mlstm_m.reshape(state_mlstm_m.shape[0], dbs, 1, ML_HEADS)

    xp = x_prompt
    n_ml = DEPTH // 2
    conv_p, conv_s, mlc_s = [], [], []
    ml_p, ml_s = (), ()
    for i in range(DEPTH):
        j = i // 2
        mg, mb = _row(ln_mix_g[i]), _row(ln_mix_b[i])
        fg, fb = _row(ln_ffn_g[i]), _row(ln_ffn_b[i])
        if i % 2 == 0:
            taps = conv_w_dw[j][:, None, :]
            args = (wpw1[j], _row(conv_b_pw1[j]), taps, _row(conv_b_dw[j]),
                    _row(conv_ln_g[j]), _row(conv_ln_b[j]), wpw2[j], _row(conv_b_pw2[j]), mg, mb)
            wide_taps = jnp.broadcast_to(taps, (CONV_WIDTH, SUBLANES, D_MODEL))
            xp, st = _conv_ffn_prompt(xp, *args[:2], wide_taps, *args[3:], w1[i], w2[i], fg, fb)
            conv_p.append(st)
            xs, st = _conv_sample(j, xs, conv_st_tm, *args)
            conv_s.append(jnp.transpose(st, (1, 0, 2)))
        else:
            wqk = jnp.concatenate([_headwise_blocks(ml_w_q[j]), _headwise_blocks(ml_w_k[j])],
                                  axis=-1).astype(BF16)
            wv = _headwise_blocks(ml_w_v[j]).astype(BF16)
            front = (winm[j], wino[j], ml_conv_w[j][:, None, :], _row(ml_conv_b[j]), wqk, wv, wif[j],
                     _row(bif[j]))
            back = (_row(ml_norm_g[j]), _row(ml_skip[j]), wdn[j], mg, mb)
            xp, *ml_p = _mlstm_prompt(j, n_ml, ml_p, xp, *front, *back)

            q, k, v, g, xc, op, st = _mlstm_sample_proj(j, xs, mlc_st_tm, *front)
            mlc_s.append(jnp.transpose(st, (1, 0, 2)))

            h_tm, *ml_s = _mlstm_sample_cell(j, n_ml, ml_s, q, k, v, g,
                                             state_mlstm_C, state_mlstm_n, m_st)
            xs = _mlstm_sample_out(xs, h_tm, xc, op, *back)
            xp = _ffn(xp.reshape(bsz * seq, D_MODEL), w1[i], w2[i], fg, fb).reshape(bsz, seq, D_MODEL)

        xs = _ffn(xs.reshape(steps * dbs, D_MODEL), w1[i], w2[i], fg, fb).reshape(steps, dbs, D_MODEL)

    y_sample = jnp.transpose(xs, (1, 0, 2))
    mlc_p, c_p, n_p, m_p = ml_p
    c_s, n_s, m_s = ml_s
    return (xp, y_sample, jnp.stack(conv_p), jnp.stack(conv_s), mlc_p, jnp.stack(mlc_s),
            c_p, c_s, n_p, n_s, m_p.reshape(n_ml, bsz, ML_HEADS), m_s.reshape(n_ml, dbs, ML_HEADS))
```

```python
import functools

import jax
import jax.numpy as jnp
from jax import lax
from jax.experimental import pallas as pl
from jax.experimental.pallas import tpu as pltpu

D_MODEL = 1024
DEPTH = 4
CONV_WIDTH = 31
D_FF = 4 * D_MODEL
D_INNER = 2 * D_MODEL
ML_HEADS = 4
ML_HEAD_DIM = D_INNER // ML_HEADS
ML_QKV_BLOCK = 4
ML_CONV_WIDTH = 4
DN_ALPHA = (2.0 * DEPTH) ** 0.25
LN_EPS = 1e-5
K_SCALE = ML_HEAD_DIM ** -0.5

LANES = 128
SUBLANES = 8
GATE_LANES = LANES
VMEM_LIMIT = 56 * 1024 * 1024

FF_CHUNK = 1024
FF_CHUNKS = D_FF // FF_CHUNK
FFN_ROWS = 512
CONV_ROWS = 256
CONV_HEAD = 32
CONV_RCHUNK = 64
CONV_RSUB = 32
ML_CHUNK = 256
ML_HEADROWS = 8
ML_RCHUNK = 32
S_BT = 32
S_TPAD = 8

BF16 = jnp.bfloat16
F32 = jnp.float32


def _const_spec(shape):
    nd = len(shape)
    return pl.BlockSpec(shape, lambda *_: (0,) * nd, pipeline_mode=pl.Buffered(1))


def _params(n_grid):
    return pltpu.CompilerParams(dimension_semantics=("arbitrary",) * n_grid,
                                vmem_limit_bytes=VMEM_LIMIT)


def _ln(x, g, b):
    mu = jnp.mean(x, axis=-1, keepdims=True)
    xc = x - mu
    var = jnp.mean(xc * xc, axis=-1, keepdims=True)
    return xc * lax.rsqrt(var + LN_EPS) * g + b


def _mm(a, w):
    return jnp.dot(a.astype(BF16), w, preferred_element_type=F32)


def _log_sigmoid(x):
    return jnp.minimum(x, 0.0) - jnp.log1p(jnp.exp(-jnp.abs(x)))


def _ffn_body(x_ref, w1_ref, w2_ref, g_ref, b_ref, o_ref):
    x = x_ref[...]
    xb = x.astype(BF16)
    acc = jnp.zeros_like(x)
    for j in range(FF_CHUNKS):
        hid = slice(j * FF_CHUNK, (j + 1) * FF_CHUNK)
        h = jnp.dot(xb, w1_ref[:, hid], preferred_element_type=F32)
        h = jnp.square(jnp.maximum(h, 0.0))
        acc = acc + jnp.dot(h.astype(BF16), w2_ref[hid, :], preferred_element_type=F32)
    o_ref[...] = _ln(DN_ALPHA * x + acc, g_ref[...], b_ref[...])


def _ffn(x2, w1, w2, g, b):
    n = x2.shape[0]
    rows = min(FFN_ROWS, n)
    return pl.pallas_call(
        _ffn_body,
        grid=(n // rows,),
        in_specs=[pl.BlockSpec((rows, D_MODEL), lambda i: (i, 0)),
                  _const_spec((D_MODEL, D_FF)), _const_spec((D_FF, D_MODEL)),
                  _const_spec((1, D_MODEL)), _const_spec((1, D_MODEL))],
        out_specs=pl.BlockSpec((rows, D_MODEL), lambda i: (i, 0)),
        out_shape=jax.ShapeDtypeStruct((n, D_MODEL), F32),
        compiler_params=_params(1),
        name="ffn",
    )(x2, w1, w2, g, b)


def _dwconv_taps_by_phase():
    first = CONV_HEAD - (CONV_WIDTH - 1)
    groups = []
    for r in range(SUBLANES):
        taps = [(k, (first + k) // SUBLANES) for k in range(CONV_WIDTH)
                if (first + k) % SUBLANES == r]
        groups.append((r, taps))
    return groups


def _conv_ffn_prompt_body(tiles_per_seq, x_ref, wpw1_ref, bpw1_ref, wdw_ref, bdw_ref, lng_ref,
                          lnb_ref, wpw2_ref, bpw2_ref, mg_ref, mb_ref, w1_ref, w2_ref, fg_ref,
                          fb_ref, y_ref, st_ref, ush_ref, cf_ref, c_ref, mid_ref, midb_ref,
                          facc_ref):
    s = pl.program_id(0)
    n_tiles = pl.num_programs(0) - 1
    t = s % tiles_per_seq
    rows = CONV_ROWS
    span = rows + CONV_HEAD

    @pl.when(s == 0)
    def _():
        mid_ref[...] = jnp.zeros(mid_ref.shape, F32)
        midb_ref[...] = jnp.zeros(midb_ref.shape, BF16)

    @pl.when(t == 0)
    def _():
        ush_ref[0, 0:CONV_HEAD, :] = jnp.zeros((CONV_HEAD, D_MODEL), F32)

    def mlp_chunk(c):
        hid = pl.ds(pl.multiple_of(c * FF_CHUNK, FF_CHUNK), FF_CHUNK)
        h = jnp.dot(midb_ref[...], w1_ref[:, hid], preferred_element_type=F32)
        h = jnp.square(jnp.maximum(h, 0.0))
        return jnp.dot(h.astype(BF16), w2_ref[hid, :], preferred_element_type=F32)

    x = x_ref[...]
    z = _mm(x, wpw1_ref[...]) + bpw1_ref[...]
    u = z[:, :D_MODEL] * jax.nn.sigmoid(z[:, D_MODEL:])
    ush_ref[0, CONV_HEAD:span, :] = u
    for r in range(1, SUBLANES):
        ush_ref[r, 0:span - SUBLANES, :] = ush_ref[0, r:r + span - SUBLANES, :]

    vregs = CONV_RSUB // SUBLANES
    groups = _dwconv_taps_by_phase()

    facc_ref[...] = jnp.zeros(facc_ref.shape, F32)

    def chunk(i, carry):
        facc_ref[...] += mlp_chunk(i)
        r0 = pl.multiple_of(i * CONV_RCHUNK, CONV_RCHUNK)
        for sub in range(CONV_RCHUNK // CONV_RSUB):
            rs = r0 + sub * CONV_RSUB
            for j in range(D_MODEL // LANES):
                cols = slice(j * LANES, (j + 1) * LANES)
                acc = None
                for r, taps in groups:
                    lo = min(a for _, a in taps)
                    hi = max(a for _, a in taps)
                    big = ush_ref[r, pl.ds(rs + lo * SUBLANES, CONV_RSUB + (hi - lo) * SUBLANES), cols]
                    for k, a in taps:
                        win = big[(a - lo) * SUBLANES:(a - lo) * SUBLANES + CONV_RSUB, :]
                        term = win.reshape(vregs, SUBLANES, LANES) * wdw_ref[k, :, cols][None]
                        acc = term if acc is None else acc + term
                cf_ref[pl.ds(rs, CONV_RSUB), cols] = acc.reshape(CONV_RSUB, LANES)
        c = _ln(cf_ref[pl.ds(r0, CONV_RCHUNK), :] + bdw_ref[...], lng_ref[...], lnb_ref[...])
        c = c * jax.nn.sigmoid(c)
        c_ref[pl.ds(r0, CONV_RCHUNK), :] = c.astype(BF16)
        return carry

    assert rows // CONV_RCHUNK == FF_CHUNKS
    lax.fori_loop(0, FF_CHUNKS, chunk, 0)

    y_ref[...] = _ln(DN_ALPHA * mid_ref[...] + facc_ref[...], fg_ref[...], fb_ref[...])
    mix = jnp.dot(c_ref[...], wpw2_ref[...], preferred_element_type=F32) + bpw2_ref[...]
    mid = _ln(DN_ALPHA * x + mix, mg_ref[...], mb_ref[...])
    mid_ref[...] = mid
    midb_ref[...] = mid.astype(BF16)

    @pl.when(jnp.logical_and(t == tiles_per_seq - 1, s < n_tiles))
    def _():
        st_ref[0] = ush_ref[0, span - (CONV_WIDTH - 1):span, :]

    ush_ref[0, 0:CONV_HEAD, :] = ush_ref[0, rows:span, :]


def _conv_ffn_prompt(x, wpw1, bpw1, wdw, bdw, lng, lnb, wpw2, bpw2, mg, mb, w1, w2, fg, fb):
    bsz, seq, _ = x.shape
    rows = CONV_ROWS
    tiles_per_seq = seq // rows
    n_tiles = bsz * tiles_per_seq

    def tile_in(s):
        return (jnp.minimum(s, n_tiles - 1), 0)

    def tile_out(s):
        return (jnp.maximum(s - 1, 0), 0)

    y, st = pl.pallas_call(
        functools.partial(_conv_ffn_prompt_body, tiles_per_seq),
        grid=(n_tiles + 1,),
        in_specs=[pl.BlockSpec((rows, D_MODEL), tile_in),
                  _const_spec((D_MODEL, 2 * D_MODEL)), _const_spec((1, 2 * D_MODEL)),
                  _const_spec((CONV_WIDTH, SUBLANES, D_MODEL)), _const_spec((1, D_MODEL)),
                  _const_spec((1, D_MODEL)), _const_spec((1, D_MODEL)),
                  _const_spec((D_MODEL, D_MODEL)), _const_spec((1, D_MODEL)),
                  _const_spec((1, D_MODEL)), _const_spec((1, D_MODEL)),
                  _const_spec((D_MODEL, D_FF)), _const_spec((D_FF, D_MODEL)),
                  _const_spec((1, D_MODEL)), _const_spec((1, D_MODEL))],
        out_specs=[pl.BlockSpec((rows, D_MODEL), tile_out),
                   pl.BlockSpec((1, CONV_WIDTH - 1, D_MODEL),
                                lambda s: (jnp.minimum(s, n_tiles - 1) // tiles_per_seq, 0, 0))],
        out_shape=[jax.ShapeDtypeStruct((bsz * seq, D_MODEL), F32),
                   jax.ShapeDtypeStruct((bsz, CONV_WIDTH - 1, D_MODEL), F32)],
        scratch_shapes=[pltpu.VMEM((SUBLANES, rows + CONV_HEAD, D_MODEL), F32),
                        pltpu.VMEM((rows, D_MODEL), F32),
                        pltpu.VMEM((rows, D_MODEL), BF16),
                        pltpu.VMEM((rows, D_MODEL), F32),
                        pltpu.VMEM((rows, D_MODEL), BF16),
                        pltpu.VMEM((rows, D_MODEL), F32)],
        compiler_params=_params(1),
        name="conv_ffn_prompt",
    )(x.reshape(bsz * seq, D_MODEL), wpw1, bpw1, wdw, bdw, lng, lnb, wpw2, bpw2, mg, mb,
      w1, w2, fg, fb)
    return y.reshape(bsz, seq, D_MODEL), st


def _conv_sample_body(layer, x_ref, st_ref, wpw1_ref, bpw1_ref, wdw_ref, bdw_ref, lng_ref, lnb_ref,
                      wpw2_ref, bpw2_ref, mg_ref, mb_ref, y_ref, nst_ref):
    del layer
    steps, bt, _ = x_ref.shape
    hist = CONV_WIDTH - 1
    x = x_ref[...].reshape(steps * bt, D_MODEL)
    z = _mm(x, wpw1_ref[...]) + bpw1_ref[...]
    u = z[:, :D_MODEL] * jax.nn.sigmoid(z[:, D_MODEL:])

    def window(j):
        if j < hist:
            return st_ref[0, j]
        return u[(j - hist) * bt:(j - hist + 1) * bt, :]

    cs = []
    for t in range(steps):
        acc = jnp.zeros((bt, D_MODEL), F32) + bdw_ref[...]
        for k in range(CONV_WIDTH):
            acc = acc + window(t + k) * wdw_ref[k]
        c = _ln(acc, lng_ref[...], lnb_ref[...])
        cs.append(c * jax.nn.sigmoid(c))
    c = jnp.concatenate(cs, axis=0)
    mix = _mm(c, wpw2_ref[...]) + bpw2_ref[...]
    y = _ln(DN_ALPHA * x + mix, mg_ref[...], mb_ref[...])
    y_ref[...] = y.reshape(steps, bt, D_MODEL)
    for j in range(hist):
        nst_ref[j] = window(j + steps)


def _conv_sample(layer, x_tm, st_tm, wpw1, bpw1, wdw, bdw, lng, lnb, wpw2, bpw2, mg, mb):
    steps, bsz, _ = x_tm.shape
    hist = CONV_WIDTH - 1
    return pl.pallas_call(
        functools.partial(_conv_sample_body, layer),
        grid=(bsz // S_BT,),
        in_specs=[pl.BlockSpec((steps, S_BT, D_MODEL), lambda i: (0, i, 0)),
                  pl.BlockSpec((1, hist, S_BT, D_MODEL), lambda i: (layer, 0, i, 0)),
                  _const_spec((D_MODEL, 2 * D_MODEL)), _const_spec((1, 2 * D_MODEL)),
                  _const_spec((CONV_WIDTH, 1, D_MODEL)), _const_spec((1, D_MODEL)),
                  _const_spec((1, D_MODEL)), _const_spec((1, D_MODEL)),
                  _const_spec((D_MODEL, D_MODEL)), _const_spec((1, D_MODEL)),
                  _const_spec((1, D_MODEL)), _const_spec((1, D_MODEL))],
        out_specs=[pl.BlockSpec((steps, S_BT, D_MODEL), lambda i: (0, i, 0)),
                   pl.BlockSpec((hist, S_BT, D_MODEL), lambda i: (0, i, 0))],
        out_shape=[jax.ShapeDtypeStruct((steps, bsz, D_MODEL), F32),
                   jax.ShapeDtypeStruct((hist, bsz, D_MODEL), F32)],
        compiler_params=_params(1),
        name="conv_sample",
    )(x_tm, st_tm, wpw1, bpw1, wdw, bdw, lng, lnb, wpw2, bpw2, mg, mb)


def _headwise_blocks(w):
    per = LANES // ML_QKV_BLOCK
    w4 = w.reshape(-1, per, ML_QKV_BLOCK, ML_QKV_BLOCK)
    eye = jnp.eye(per, dtype=w.dtype)
    return jnp.einsum('jaio,ab->jaibo', w4, eye).reshape(-1, LANES, LANES)


def _headwise_qkv(xc, xm, wqk_ref, wv_ref, j):
    cols = slice(j * LANES, (j + 1) * LANES)
    qk = jnp.dot(xc[:, cols], wqk_ref[j], preferred_element_type=F32)
    v = jnp.dot(xm[:, cols], wv_ref[j], preferred_element_type=F32)
    return qk[:, :LANES], qk[:, LANES:], v


def _gates(q, k, v, wif_ref, bif_ref):
    g = _mm(q, wif_ref[0:D_INNER, :])
    g = g + _mm(k, wif_ref[D_INNER:2 * D_INNER, :])
    g = g + _mm(v, wif_ref[2 * D_INNER:3 * D_INNER, :])
    return g + bif_ref[...]


def _cell_head(qh, qb, kh, vb, igc, lfc, c_old, n_old, m_old):
    length = qh.shape[0]
    row = lax.broadcasted_iota(jnp.int32, (length, length), 0)
    col = lax.broadcasted_iota(jnp.int32, (length, length), 1)
    causal = col <= row
    eye = col == row

    def to_row(c):
        return jnp.sum(jnp.where(eye, c, 0.0), axis=0, keepdims=True)

    lfr = to_row(lfc)
    bc = jnp.sum(jnp.where(causal, lfr, 0.0), axis=1, keepdims=True)
    br = to_row(bc)
    igr = to_row(igc)
    d = jnp.where(causal, bc - br + igr, -jnp.inf)
    inter = bc + m_old
    m_t = jnp.maximum(inter, jnp.max(d, axis=1, keepdims=True))
    s = lax.dot_general(qb, kh.astype(BF16), (((1,), (1,)), ((), ())), preferred_element_type=F32)
    p = s * jnp.exp(d - m_t)
    a = jnp.exp(inter - m_t)
    num = a * jnp.dot(qb, c_old.astype(BF16), preferred_element_type=F32)
    num = num + jnp.dot(p.astype(BF16), vb, preferred_element_type=F32)
    den = a * jnp.sum(qh * n_old, axis=1, keepdims=True) + jnp.sum(p, axis=1, keepdims=True)
    h = num * (1.0 / jnp.maximum(jnp.abs(den), jnp.exp(-m_t)))

    b_last = bc[length - 1:length, :]
    gc = b_last - bc + igc
    m_new = jnp.maximum(b_last + m_old, jnp.max(gc, axis=0, keepdims=True))
    decay = jnp.exp(b_last + m_old - m_new)
    wk = kh * jnp.exp(gc - m_new)
    c_new = decay * c_old + lax.dot_general(wk.astype(BF16), vb, (((0,), (0,)), ((), ())),
                                            preferred_element_type=F32)
    n_new = decay * n_old + jnp.sum(wk, axis=0, keepdims=True)
    return h, c_new, n_new, m_new


def _head_out(h, o_pre, xc, ng, sk):
    mu = jnp.mean(h, axis=-1, keepdims=True)
    hc = h - mu
    var = jnp.mean(hc * hc, axis=-1, keepdims=True)
    hn = hc * lax.rsqrt(var + LN_EPS) * ng
    return jax.nn.sigmoid(o_pre) * (hn + sk * xc)


def _mlstm_prompt_body(n_prev, x_ref, win_ref, cw_ref, cb_ref, wqk_ref, wv_ref, wif_ref,
                       bif_ref, ng_ref, sk_ref, wdn_ref, mg_ref, mb_ref, *rest):
    (y_ref, mlc_ref, c_all_ref, n_all_ref, m_all_ref,
     xbuf_ref, op_ref, xc_ref, q_ref, k_ref, xb_ref, xcb_ref, xmb_ref, qb_ref, kb_ref,
     vb_ref) = rest[n_prev:]
    c_ref, n_ref, m_ref = c_all_ref.at[0], n_all_ref.at[0], m_all_ref.at[0]
    t = pl.program_id(1)
    rows = ML_CHUNK
    hist = ML_CONV_WIDTH - 1

    @pl.when(t == 0)
    def _():
        xbuf_ref[0:ML_HEADROWS, :] = jnp.zeros((ML_HEADROWS, D_INNER), F32)
        c_ref[...] = jnp.zeros(c_ref.shape, F32)
        n_ref[...] = jnp.zeros(n_ref.shape, F32)
        m_ref[...] = jnp.zeros(m_ref.shape, F32)

    x = x_ref[0]
    xb_ref[...] = x.astype(BF16)
    xbuf_ref[ML_HEADROWS:ML_HEADROWS + rows, :] = jnp.dot(xb_ref[...], win_ref[:, :D_INNER],
                                                          preferred_element_type=F32)
    op_ref[...] = jnp.dot(xb_ref[...], win_ref[:, D_INNER:], preferred_element_type=F32)

    for j in range(D_INNER // LANES):
        cols = slice(j * LANES, (j + 1) * LANES)
        for i in range(rows // ML_RCHUNK):
            r0 = i * ML_RCHUNK
            out_rows = slice(r0, r0 + ML_RCHUNK)
            win = xbuf_ref[r0:r0 + ML_RCHUNK + ML_HEADROWS, cols]
            xm = win[ML_HEADROWS:, :]
            acc = xm * cw_ref[hist, :, cols] + cb_ref[:, cols]
            for k in range(hist):
                off = ML_HEADROWS - hist + k
                acc = acc + win[off:off + ML_RCHUNK, :] * cw_ref[k, :, cols]
            xc = acc * jax.nn.sigmoid(acc)
            xc_ref[out_rows, cols] = xc
            xcb_ref[out_rows, cols] = xc.astype(BF16)
            xmb_ref[out_rows, cols] = xm.astype(BF16)
        q, k, v = _headwise_qkv(xcb_ref, xmb_ref, wqk_ref, wv_ref, j)
        q_ref[:, cols] = q
        k_ref[:, cols] = k
        qb_ref[:, cols] = q.astype(BF16)
        kb_ref[:, cols] = k.astype(BF16)
        vb_ref[:, cols] = v.astype(BF16)

    @pl.when(t == pl.num_programs(1) - 1)
    def _():
        mlc_ref[0, 0] = xbuf_ref[ML_HEADROWS + rows - hist:ML_HEADROWS + rows, :]

    xbuf_ref[0:ML_HEADROWS, :] = xbuf_ref[rows:rows + ML_HEADROWS, :]

    gates = jnp.dot(qb_ref[...], wif_ref[0:D_INNER, :], preferred_element_type=F32)
    gates = gates + jnp.dot(kb_ref[...], wif_ref[D_INNER:2 * D_INNER, :], preferred_element_type=F32)
    gates = gates + jnp.dot(vb_ref[...], wif_ref[2 * D_INNER:, :], preferred_element_type=F32)
    gates = gates + bif_ref[...]

    mix = jnp.zeros((rows, D_MODEL), F32)
    for h in range(ML_HEADS):
        cols = slice(h * ML_HEAD_DIM, (h + 1) * ML_HEAD_DIM)
        igc = gates[:, h:h + 1]
        lfc = _log_sigmoid(gates[:, ML_HEADS + h:ML_HEADS + h + 1])
        hh, c_new, n_new, m_new = _cell_head(q_ref[:, cols], qb_ref[:, cols],
                                             k_ref[:, cols] * K_SCALE, vb_ref[:, cols], igc, lfc,
                                             c_ref[0, h], n_ref[0, h:h + 1, :], m_ref[0, :, h:h + 1])
        c_ref[0, h] = c_new
        n_ref[0, h:h + 1, :] = n_new
        m_ref[0, :, h:h + 1] = m_new
        out = _head_out(hh, op_ref[:, cols], xc_ref[:, cols], ng_ref[:, cols], sk_ref[:, cols])
        mix = mix + _mm(out, wdn_ref[cols, :])
    y_ref[0] = _ln(DN_ALPHA * x + mix, mg_ref[...], mb_ref[...])


def _stacked_state(layer, n_layers, prev, shapes, n_inputs, first_out):
    def spec(shape):
        nd = len(shape)
        return pl.BlockSpec((1, 1) + shape[1:], lambda b, *_: (layer, b) + (0,) * (nd - 1))

    out_specs = [spec(s) for s in shapes]
    out_shapes = [jax.ShapeDtypeStruct((n_layers,) + s, F32) for s in shapes]
    prev_specs = [pl.BlockSpec(memory_space=pl.ANY) for _ in prev]
    aliases = {n_inputs + i: first_out + i for i in range(len(prev))}
    return out_specs, out_shapes, prev_specs, aliases


def _mlstm_prompt(layer, n_layers, prev, x, win, cw, cb, wqk, wv, wif, bif, ng, sk, wdn, mg, mb):
    bsz, seq, _ = x.shape
    rows = ML_CHUNK
    hist = ML_CONV_WIDTH - 1
    tiles = D_INNER // LANES
    weights = (win, cw, cb, wqk, wv, wif, bif, ng, sk, wdn, mg, mb)
    st_specs, st_shapes, prev_specs, aliases = _stacked_state(
        layer, n_layers, prev,
        [(bsz, hist, D_INNER), (bsz, ML_HEADS, ML_HEAD_DIM, ML_HEAD_DIM),
         (bsz, ML_HEADS, ML_HEAD_DIM), (bsz, 1, ML_HEADS)],
        n_inputs=1 + len(weights), first_out=1)
    return pl.pallas_call(
        functools.partial(_mlstm_prompt_body, len(prev)),
        grid=(bsz, seq // rows),
        in_specs=[pl.BlockSpec((1, rows, D_MODEL), lambda b, t: (b, t, 0)),
                  _const_spec((D_MODEL, 2 * D_INNER)),
                  _const_spec((ML_CONV_WIDTH, 1, D_INNER)), _const_spec((1, D_INNER)),
                  _const_spec((tiles, LANES, 2 * LANES)), _const_spec((tiles, LANES, LANES)),
                  _const_spec((3 * D_INNER, GATE_LANES)), _const_spec((1, GATE_LANES)),
                  _const_spec((1, D_INNER)), _const_spec((1, D_INNER)),
                  _const_spec((D_INNER, D_MODEL)),
                  _const_spec((1, D_MODEL)), _const_spec((1, D_MODEL))] + prev_specs,
        out_specs=[pl.BlockSpec((1, rows, D_MODEL), lambda b, t: (b, t, 0))] + st_specs,
        out_shape=[jax.ShapeDtypeStruct((bsz, seq, D_MODEL), F32)] + st_shapes,
        input_output_aliases=aliases,
        scratch_shapes=[pltpu.VMEM((ML_HEADROWS + rows, D_INNER), F32),
                        pltpu.VMEM((rows, D_INNER), F32),
                        pltpu.VMEM((rows, D_INNER), F32),
                        pltpu.VMEM((rows, D_INNER), F32),
                        pltpu.VMEM((rows, D_INNER), F32),
                        pltpu.VMEM((rows, D_MODEL), BF16),
                        pltpu.VMEM((rows, D_INNER), BF16),
                        pltpu.VMEM((rows, D_INNER), BF16),
                        pltpu.VMEM((rows, D_INNER), BF16),
                        pltpu.VMEM((rows, D_INNER), BF16),
                        pltpu.VMEM((rows, D_INNER), BF16)],
        compiler_params=_params(2),
        name="mlstm_prompt",
    )(x, *weights, *prev)


def _mlstm_sample_proj_body(layer, x_ref, st_ref, win_ref, cw_ref, cb_ref, wqk_ref, wv_ref,
                            wif_ref, bif_ref, q_ref, k_ref, v_ref, g_ref, xc_ref, op_ref, nst_ref):
    del layer
    steps, bt, _ = x_ref.shape
    hist = ML_CONV_WIDTH - 1
    x = x_ref[...].reshape(steps * bt, D_MODEL)
    z = _mm(x, win_ref[...])
    xm = z[:, :D_INNER]
    op_ref[...] = z[:, D_INNER:].reshape(steps, bt, D_INNER)

    def window(j):
        if j < hist:
            return st_ref[0, j]
        return xm[(j - hist) * bt:(j - hist + 1) * bt, :]

    xcs = []
    for t in range(steps):
        acc = jnp.zeros((bt, D_INNER), F32) + cb_ref[...]
        for k in range(ML_CONV_WIDTH):
            acc = acc + window(t + k) * cw_ref[k]
        xcs.append(acc * jax.nn.sigmoid(acc))
    xc = jnp.concatenate(xcs, axis=0)
    for j in range(hist):
        nst_ref[j] = window(j + steps)
    xcb = xc.astype(BF16)
    xmb = xm.astype(BF16)
    tiles = [_headwise_qkv(xcb, xmb, wqk_ref, wv_ref, j) for j in range(D_INNER // LANES)]
    q, k, v = (jnp.concatenate(part, axis=1) for part in zip(*tiles))
    gates = _gates(q, k, v, wif_ref, bif_ref)
    q_ref[...] = q.reshape(steps, bt, D_INNER)
    k_ref[...] = k.reshape(steps, bt, D_INNER)
    v_ref[...] = v.reshape(steps, bt, D_INNER)
    g_ref[...] = gates.reshape(steps, bt, GATE_LANES)
    xc_ref[...] = xc.reshape(steps, bt, D_INNER)


def _mlstm_sample_proj(layer, x_tm, st_tm, win, cw, cb, wqk, wv, wif, bif):
    steps, bsz, _ = x_tm.shape
    hist = ML_CONV_WIDTH - 1
    tiles = D_INNER // LANES
    wide = pl.BlockSpec((steps, S_BT, D_INNER), lambda i: (0, i, 0))
    wide_shape = jax.ShapeDtypeStruct((steps, bsz, D_INNER), F32)
    return pl.pallas_call(
        functools.partial(_mlstm_sample_proj_body, layer),
        grid=(bsz // S_BT,),
        in_specs=[pl.BlockSpec((steps, S_BT, D_MODEL), lambda i: (0, i, 0)),
                  pl.BlockSpec((1, hist, S_BT, D_INNER), lambda i: (layer, 0, i, 0)),
                  _const_spec((D_MODEL, 2 * D_INNER)),
                  _const_spec((ML_CONV_WIDTH, 1, D_INNER)), _const_spec((1, D_INNER)),
                  _const_spec((tiles, LANES, 2 * LANES)), _const_spec((tiles, LANES, LANES)),
                  _const_spec((3 * D_INNER, GATE_LANES)), _const_spec((1, GATE_LANES))],
        out_specs=[wide, wide, wide,
                   pl.BlockSpec((steps, S_BT, GATE_LANES), lambda i: (0, i, 0)),
                   wide, wide,
                   pl.BlockSpec((hist, S_BT, D_INNER), lambda i: (0, i, 0))],
        out_shape=[wide_shape, wide_shape, wide_shape,
                   jax.ShapeDtypeStruct((steps, bsz, GATE_LANES), F32),
                   wide_shape, wide_shape,
                   jax.ShapeDtypeStruct((hist, bsz, D_INNER), F32)],
        compiler_params=_params(1),
        name="mlstm_sample_proj",
    )(x_tm, st_tm, win, cw, cb, wqk, wv, wif, bif)


def _mlstm_sample_cell_body(steps, n_prev, q_ref, k_ref, v_ref, g_ref, c_ref, n_ref, m_ref, *rest):
    h_ref, co_ref, no_ref, mo_ref = rest[n_prev:]
    lane = pl.program_id(0) % SUBLANES
    sub = lax.broadcasted_iota(jnp.int32, (SUBLANES, 1), 0)
    mine = sub == lane

    def gather(ref, cols):
        out = jnp.zeros((S_TPAD, cols.stop - cols.start), F32)
        for t in range(steps):
            row = jnp.sum(jnp.where(mine, ref[t, :, cols], 0.0), axis=0, keepdims=True)
            out = jnp.where(sub == t, row, out)
        return out

    @pl.when(lane == 0)
    def _():
        h_ref[...] = jnp.zeros(h_ref.shape, F32)

    gates = gather(g_ref, slice(0, GATE_LANES))
    valid = sub < steps
    for h in range(ML_HEADS):
        cols = slice(h * ML_HEAD_DIM, (h + 1) * ML_HEAD_DIM)
        igc = jnp.where(valid, gates[:, h:h + 1], -jnp.inf)
        lfc = jnp.where(valid, _log_sigmoid(gates[:, ML_HEADS + h:ML_HEADS + h + 1]), 0.0)
        qh = gather(q_ref, cols)
        hh, c_new, n_new, m_new = _cell_head(qh, qh.astype(BF16), gather(k_ref, cols) * K_SCALE,
                                             gather(v_ref, cols).astype(BF16), igc, lfc,
                                             c_ref[0, 0, h], n_ref[0, 0, h:h + 1, :],
                                             m_ref[0, 0, :, h:h + 1])
        for t in range(steps):
            h_ref[t, :, cols] = jnp.where(mine, hh[t:t + 1, :], h_ref[t, :, cols])
        co_ref[0, 0, h] = c_new
        no_ref[0, 0, h:h + 1, :] = n_new
        mo_ref[0, 0, :, h:h + 1] = m_new


def _mlstm_sample_cell(layer, n_layers, prev, q_tm, k_tm, v_tm, g_tm, c_st, n_st, m_st):
    steps, bsz, _ = q_tm.shape
    wide = pl.BlockSpec((steps, SUBLANES, D_INNER), lambda b: (0, b // SUBLANES, 0))
    st_specs, st_shapes, prev_specs, aliases = _stacked_state(
        layer, n_layers, prev,
        [(bsz, ML_HEADS, ML_HEAD_DIM, ML_HEAD_DIM), (bsz, ML_HEADS, ML_HEAD_DIM),
         (bsz, 1, ML_HEADS)],
        n_inputs=7, first_out=1)
    return pl.pallas_call(
        functools.partial(_mlstm_sample_cell_body, steps, len(prev)),
        grid=(bsz,),
        in_specs=[wide, wide, wide,
                  pl.BlockSpec((steps, SUBLANES, GATE_LANES), lambda b: (0, b // SUBLANES, 0)),
                  pl.BlockSpec((1, 1, ML_HEADS, ML_HEAD_DIM, ML_HEAD_DIM),
                               lambda b: (layer, b, 0, 0, 0)),
                  pl.BlockSpec((1, 1, ML_HEADS, ML_HEAD_DIM), lambda b: (layer, b, 0, 0)),
                  pl.BlockSpec((1, 1, 1, ML_HEADS), lambda b: (layer, b, 0, 0))] + prev_specs,
        out_specs=[wide] + st_specs,
        out_shape=[jax.ShapeDtypeStruct((steps, bsz, D_INNER), F32)] + st_shapes,
        input_output_aliases=aliases,
        compiler_params=_params(1),
        name="mlstm_sample_cell",
    )(q_tm, k_tm, v_tm, g_tm, c_st, n_st, m_st, *prev)


def _mlstm_sample_out_body(x_ref, h_ref, xc_ref, op_ref, ng_ref, sk_ref, wdn_ref, mg_ref, mb_ref,
                           y_ref):
    steps, bt, _ = x_ref.shape
    n = steps * bt
    x = x_ref[...].reshape(n, D_MODEL)
    mix = jnp.zeros((n, D_MODEL), F32)
    for h in range(ML_HEADS):
        cols = slice(h * ML_HEAD_DIM, (h + 1) * ML_HEAD_DIM)
        out = _head_out(h_ref[:, :, cols].reshape(n, ML_HEAD_DIM),
                        op_ref[:, :, cols].reshape(n, ML_HEAD_DIM),
                        xc_ref[:, :, cols].reshape(n, ML_HEAD_DIM),
                        ng_ref[:, cols], sk_ref[:, cols])
        mix = mix + _mm(out, wdn_ref[cols, :])
    y_ref[...] = _ln(DN_ALPHA * x + mix, mg_ref[...], mb_ref[...]).reshape(steps, bt, D_MODEL)


def _mlstm_sample_out(x_tm, h_tm, xc_tm, op_tm, ng, sk, wdn, mg, mb):
    steps, bsz, _ = x_tm.shape
    wide = pl.BlockSpec((steps, S_BT, D_INNER), lambda i: (0, i, 0))
    narrow = pl.BlockSpec((steps, S_BT, D_MODEL), lambda i: (0, i, 0))
    return pl.pallas_call(
        _mlstm_sample_out_body,
        grid=(bsz // S_BT,),
        in_specs=[narrow, wide, wide, wide,
                  _const_spec((1, D_INNER)), _const_spec((1, D_INNER)),
                  _const_spec((D_INNER, D_MODEL)),
                  _const_spec((1, D_MODEL)), _const_spec((1, D_MODEL))],
        out_specs=narrow,
        out_shape=jax.ShapeDtypeStruct((steps, bsz, D_MODEL), F32),
        compiler_params=_params(1),
        name="mlstm_sample_out",
    )(x_tm, h_tm, xc_tm, op_tm, ng, sk, wdn, mg, mb)


def _row(v):
    return v.reshape(1, -1)


def kernel(x_prompt, x_sample, state_conv, state_mlstm_conv, state_mlstm_C, state_mlstm_n, state_mlstm_m, conv_w_pw1, conv_b_pw1, conv_w_dw, conv_b_dw, conv_ln_g, conv_ln_b, conv_w_pw2, conv_b_pw2, ml_w_in, ml_conv_w, ml_conv_b, ml_w_q, ml_w_k, ml_w_v, ml_w_if, ml_b_if, ml_norm_g, ml_skip, ml_w_down, ln_mix_g, ln_mix_b, ffn_w1, ffn_w2, ln_ffn_g, ln_ffn_b):
    bsz, seq, _ = x_prompt.shape
    dbs, steps, _ = x_sample.shape

    def per_layer(w):
        return [w[layer].astype(BF16) for layer in range(w.shape[0])]

    wpw1, wpw2, win, wdn = (per_layer(w) for w in (conv_w_pw1, conv_w_pw2, ml_w_in, ml_w_down))
    w1, w2 = per_layer(ffn_w1), per_layer(ffn_w2)
    wif = per_layer(jnp.pad(ml_w_if, ((0, 0), (0, 0), (0, GATE_LANES - 2 * ML_HEADS))))
    bif = jnp.pad(ml_b_if, ((0, 0), (0, GATE_LANES - 2 * ML_HEADS)))

    xs = jnp.transpose(x_sample, (1, 0, 2))
    conv_st_tm = jnp.transpose(state_conv, (0, 2, 1, 3))
    mlc_st_tm = jnp.transpose(state_mlstm_conv, (0, 2, 1, 3))
    m_st = state_mlstm_m.reshape(state_mlstm_m.shape[0], dbs, 1, ML_HEADS)

    xp = x_prompt
    n_ml = DEPTH // 2
    conv_p, conv_s, mlc_s = [], [], []
    ml_p, ml_s = (), ()
    for i in range(DEPTH):
        j = i // 2
        mg, mb = _row(ln_mix_g[i]), _row(ln_mix_b[i])
        fg, fb = _row(ln_ffn_g[i]), _row(ln_ffn_b[i])
        if i % 2 == 0:
            taps = conv_w_dw[j][:, None, :]
            args = (wpw1[j], _row(conv_b_pw1[j]), taps, _row(conv_b_dw[j]),
                    _row(conv_ln_g[j]), _row(conv_ln_b[j]), wpw2[j], _row(conv_b_pw2[j]), mg, mb)
            wide_taps = jnp.broadcast_to(taps, (CONV_WIDTH, SUBLANES, D_MODEL))
            xp, st = _conv_ffn_prompt(xp, *args[:2], wide_taps, *args[3:], w1[i], w2[i], fg, fb)
            conv_p.append(st)
            xs, st = _conv_sample(j, xs, conv_st_tm, *args)
            conv_s.append(jnp.transpose(st, (1, 0, 2)))
        else:
            wqk = jnp.concatenate([_headwise_blocks(ml_w_q[j]), _headwise_blocks(ml_w_k[j])],
                                  axis=-1).astype(BF16)
            wv = _headwise_blocks(ml_w_v[j]).astype(BF16)
            front = (win[j], ml_conv_w[j][:, None, :], _row(ml_conv_b[j]), wqk, wv, wif[j],
                     _row(bif[j]))
            back = (_row(ml_norm_g[j]), _row(ml_skip[j]), wdn[j], mg, mb)
            xp, *ml_p = _mlstm_prompt(j, n_ml, ml_p, xp, *front, *back)

            q, k, v, g, xc, op, st = _mlstm_sample_proj(j, xs, mlc_st_tm, *front)
            mlc_s.append(jnp.transpose(st, (1, 0, 2)))

            h_tm, *ml_s = _mlstm_sample_cell(j, n_ml, ml_s, q, k, v, g,
                                             state_mlstm_C, state_mlstm_n, m_st)
            xs = _mlstm_sample_out(xs, h_tm, xc, op, *back)
            xp = _ffn(xp.reshape(bsz * seq, D_MODEL), w1[i], w2[i], fg, fb).reshape(bsz, seq, D_MODEL)

        xs = _ffn(xs.reshape(steps * dbs, D_MODEL), w1[i], w2[i], fg, fb).reshape(steps, dbs, D_MODEL)

    y_sample = jnp.transpose(xs, (1, 0, 2))
    mlc_p, c_p, n_p, m_p = ml_p
    c_s, n_s, m_s = ml_s
    return (xp, y_sample, jnp.stack(conv_p), jnp.stack(conv_s), mlc_p, jnp.stack(mlc_s),
            c_p, c_s, n_p, n_s, m_p.reshape(n_ml, bsz, ML_HEADS), m_s.reshape(n_ml, dbs, ML_HEADS))
```

```python
import functools

import jax
import jax.numpy as jnp
from jax import lax
from jax.experimental import pallas as pl
from jax.experimental.pallas import tpu as pltpu

D_MODEL = 1024
DEPTH = 4
CONV_WIDTH = 31
D_FF = 4 * D_MODEL
D_INNER = 2 * D_MODEL
ML_HEADS = 4
ML_HEAD_DIM = D_INNER // ML_HEADS
ML_QKV_BLOCK = 4
ML_CONV_WIDTH = 4
DN_ALPHA = (2.0 * DEPTH) ** 0.25
LN_EPS = 1e-5
K_SCALE = ML_HEAD_DIM ** -0.5

LANES = 128
SUBLANES = 8
GATE_LANES = LANES
VMEM_LIMIT = 56 * 1024 * 1024

FF_CHUNK = 1024
FF_CHUNKS = D_FF // FF_CHUNK
FFN_ROWS = 512
CONV_ROWS = 256
CONV_HEAD = 32
CONV_RCHUNK = 64
CONV_RSUB = 32
ML_CHUNK = 256
ML_HEADROWS = 8
ML_RCHUNK = 32
S_BT = 32
S_SEQ = 2
S_TPAD = 8

BF16 = jnp.bfloat16
F32 = jnp.float32


def _const_spec(shape):
    nd = len(shape)
    return pl.BlockSpec(shape, lambda *_: (0,) * nd, pipeline_mode=pl.Buffered(1))


def _params(n_grid):
    return pltpu.CompilerParams(dimension_semantics=("arbitrary",) * n_grid,
                                vmem_limit_bytes=VMEM_LIMIT)


def _ln(x, g, b):
    mu = jnp.mean(x, axis=-1, keepdims=True)
    xc = x - mu
    var = jnp.mean(xc * xc, axis=-1, keepdims=True)
    return xc * lax.rsqrt(var + LN_EPS) * g + b


def _mm(a, w):
    return jnp.dot(a.astype(BF16), w, preferred_element_type=F32)


def _log_sigmoid(x):
    return jnp.minimum(x, 0.0) - jnp.log1p(jnp.exp(-jnp.abs(x)))


def _ffn_body(x_ref, w1_ref, w2_ref, g_ref, b_ref, o_ref):
    x = x_ref[...]
    xb = x.astype(BF16)
    acc = jnp.zeros_like(x)
    for j in range(FF_CHUNKS):
        hid = slice(j * FF_CHUNK, (j + 1) * FF_CHUNK)
        h = jnp.dot(xb, w1_ref[:, hid], preferred_element_type=F32)
        h = jnp.square(jnp.maximum(h, 0.0))
        acc = acc + jnp.dot(h.astype(BF16), w2_ref[hid, :], preferred_element_type=F32)
    o_ref[...] = _ln(DN_ALPHA * x + acc, g_ref[...], b_ref[...])


def _ffn(x2, w1, w2, g, b):
    n = x2.shape[0]
    rows = min(FFN_ROWS, n)
    return pl.pallas_call(
        _ffn_body,
        grid=(n // rows,),
        in_specs=[pl.BlockSpec((rows, D_MODEL), lambda i: (i, 0)),
                  _const_spec((D_MODEL, D_FF)), _const_spec((D_FF, D_MODEL)),
                  _const_spec((1, D_MODEL)), _const_spec((1, D_MODEL))],
        out_specs=pl.BlockSpec((rows, D_MODEL), lambda i: (i, 0)),
        out_shape=jax.ShapeDtypeStruct((n, D_MODEL), F32),
        compiler_params=_params(1),
        name="ffn",
    )(x2, w1, w2, g, b)


def _dwconv_taps_by_phase():
    first = CONV_HEAD - (CONV_WIDTH - 1)
    groups = []
    for r in range(SUBLANES):
        taps = [(k, (first + k) // SUBLANES) for k in range(CONV_WIDTH)
                if (first + k) % SUBLANES == r]
        groups.append((r, taps))
    return groups


def _conv_ffn_prompt_body(tiles_per_seq, x_ref, wpw1_ref, bpw1_ref, wdw_ref, bdw_ref, lng_ref,
                          lnb_ref, wpw2_ref, bpw2_ref, mg_ref, mb_ref, w1_ref, w2_ref, fg_ref,
                          fb_ref, y_ref, st_ref, ush_ref, cf_ref, c_ref, mid_ref, midb_ref,
                          facc_ref):
    s = pl.program_id(0)
    n_tiles = pl.num_programs(0) - 1
    t = s % tiles_per_seq
    rows = CONV_ROWS
    span = rows + CONV_HEAD

    @pl.when(s == 0)
    def _():
        mid_ref[...] = jnp.zeros(mid_ref.shape, F32)
        midb_ref[...] = jnp.zeros(midb_ref.shape, BF16)

    @pl.when(t == 0)
    def _():
        ush_ref[0, 0:CONV_HEAD, :] = jnp.zeros((CONV_HEAD, D_MODEL), F32)

    def mlp_chunk(c):
        hid = pl.ds(pl.multiple_of(c * FF_CHUNK, FF_CHUNK), FF_CHUNK)
        h = jnp.dot(midb_ref[...], w1_ref[:, hid], preferred_element_type=F32)
        h = jnp.square(jnp.maximum(h, 0.0))
        return jnp.dot(h.astype(BF16), w2_ref[hid, :], preferred_element_type=F32)

    x = x_ref[...]
    z = _mm(x, wpw1_ref[...]) + bpw1_ref[...]
    u = z[:, :D_MODEL] * jax.nn.sigmoid(z[:, D_MODEL:])
    ush_ref[0, CONV_HEAD:span, :] = u
    for r in range(1, SUBLANES):
        ush_ref[r, 0:span - SUBLANES, :] = ush_ref[0, r:r + span - SUBLANES, :]

    vregs = CONV_RSUB // SUBLANES
    groups = _dwconv_taps_by_phase()

    facc_ref[...] = jnp.zeros(facc_ref.shape, F32)

    def chunk(i, carry):
        facc_ref[...] += mlp_chunk(i)
        r0 = pl.multiple_of(i * CONV_RCHUNK, CONV_RCHUNK)
        for sub in range(CONV_RCHUNK // CONV_RSUB):
            rs = r0 + sub * CONV_RSUB
            for j in range(D_MODEL // LANES):
                cols = slice(j * LANES, (j + 1) * LANES)
                acc = None
                for r, taps in groups:
                    lo = min(a for _, a in taps)
                    hi = max(a for _, a in taps)
                    big = ush_ref[r, pl.ds(rs + lo * SUBLANES, CONV_RSUB + (hi - lo) * SUBLANES), cols]
                    for k, a in taps:
                        win = big[(a - lo) * SUBLANES:(a - lo) * SUBLANES + CONV_RSUB, :]
                        term = win.reshape(vregs, SUBLANES, LANES) * wdw_ref[k, :, cols][None]
                        acc = term if acc is None else acc + term
                cf_ref[pl.ds(rs, CONV_RSUB), cols] = acc.reshape(CONV_RSUB, LANES)
        c = _ln(cf_ref[pl.ds(r0, CONV_RCHUNK), :] + bdw_ref[...], lng_ref[...], lnb_ref[...])
        c = c * jax.nn.sigmoid(c)
        c_ref[pl.ds(r0, CONV_RCHUNK), :] = c.astype(BF16)
        return carry

    assert rows // CONV_RCHUNK == FF_CHUNKS
    lax.fori_loop(0, FF_CHUNKS, chunk, 0)

    y_ref[...] = _ln(DN_ALPHA * mid_ref[...] + facc_ref[...], fg_ref[...], fb_ref[...])
    mix = jnp.dot(c_ref[...], wpw2_ref[...], preferred_element_type=F32) + bpw2_ref[...]
    mid = _ln(DN_ALPHA * x + mix, mg_ref[...], mb_ref[...])
    mid_ref[...] = mid
    midb_ref[...] = mid.astype(BF16)

    @pl.when(jnp.logical_and(t == tiles_per_seq - 1, s < n_tiles))
    def _():
        st_ref[0] = ush_ref[0, span - (CONV_WIDTH - 1):span, :]

    ush_ref[0, 0:CONV_HEAD, :] = ush_ref[0, rows:span, :]


def _conv_ffn_prompt(x, wpw1, bpw1, wdw, bdw, lng, lnb, wpw2, bpw2, mg, mb, w1, w2, fg, fb):
    bsz, seq, _ = x.shape
    rows = CONV_ROWS
    tiles_per_seq = seq // rows
    n_tiles = bsz * tiles_per_seq

    def tile_in(s):
        return (jnp.minimum(s, n_tiles - 1), 0)

    def tile_out(s):
        return (jnp.maximum(s - 1, 0), 0)

    y, st = pl.pallas_call(
        functools.partial(_conv_ffn_prompt_body, tiles_per_seq),
        grid=(n_tiles + 1,),
        in_specs=[pl.BlockSpec((rows, D_MODEL), tile_in),
                  _const_spec((D_MODEL, 2 * D_MODEL)), _const_spec((1, 2 * D_MODEL)),
                  _const_spec((CONV_WIDTH, SUBLANES, D_MODEL)), _const_spec((1, D_MODEL)),
                  _const_spec((1, D_MODEL)), _const_spec((1, D_MODEL)),
                  _const_spec((D_MODEL, D_MODEL)), _const_spec((1, D_MODEL)),
                  _const_spec((1, D_MODEL)), _const_spec((1, D_MODEL)),
                  _const_spec((D_MODEL, D_FF)), _const_spec((D_FF, D_MODEL)),
                  _const_spec((1, D_MODEL)), _const_spec((1, D_MODEL))],
        out_specs=[pl.BlockSpec((rows, D_MODEL), tile_out),
                   pl.BlockSpec((1, CONV_WIDTH - 1, D_MODEL),
                                lambda s: (jnp.minimum(s, n_tiles - 1) // tiles_per_seq, 0, 0))],
        out_shape=[jax.ShapeDtypeStruct((bsz * seq, D_MODEL), F32),
                   jax.ShapeDtypeStruct((bsz, CONV_WIDTH - 1, D_MODEL), F32)],
        scratch_shapes=[pltpu.VMEM((SUBLANES, rows + CONV_HEAD, D_MODEL), F32),
                        pltpu.VMEM((rows, D_MODEL), F32),
                        pltpu.VMEM((rows, D_MODEL), BF16),
                        pltpu.VMEM((rows, D_MODEL), F32),
                        pltpu.VMEM((rows, D_MODEL), BF16),
                        pltpu.VMEM((rows, D_MODEL), F32)],
        compiler_params=_params(1),
        name="conv_ffn_prompt",
    )(x.reshape(bsz * seq, D_MODEL), wpw1, bpw1, wdw, bdw, lng, lnb, wpw2, bpw2, mg, mb,
      w1, w2, fg, fb)
    return y.reshape(bsz, seq, D_MODEL), st


def _conv_sample_body(layer, x_ref, st_ref, wpw1_ref, bpw1_ref, wdw_ref, bdw_ref, lng_ref, lnb_ref,
                      wpw2_ref, bpw2_ref, mg_ref, mb_ref, y_ref, nst_ref):
    del layer
    steps, bt, _ = x_ref.shape
    hist = CONV_WIDTH - 1
    x = x_ref[...].reshape(steps * bt, D_MODEL)
    z = _mm(x, wpw1_ref[...]) + bpw1_ref[...]
    u = z[:, :D_MODEL] * jax.nn.sigmoid(z[:, D_MODEL:])

    def window(j):
        if j < hist:
            return st_ref[0, j]
        return u[(j - hist) * bt:(j - hist + 1) * bt, :]

    cs = []
    for t in range(steps):
        acc = jnp.zeros((bt, D_MODEL), F32) + bdw_ref[...]
        for k in range(CONV_WIDTH):
            acc = acc + window(t + k) * wdw_ref[k]
        c = _ln(acc, lng_ref[...], lnb_ref[...])
        cs.append(c * jax.nn.sigmoid(c))
    c = jnp.concatenate(cs, axis=0)
    mix = _mm(c, wpw2_ref[...]) + bpw2_ref[...]
    y = _ln(DN_ALPHA * x + mix, mg_ref[...], mb_ref[...])
    y_ref[...] = y.reshape(steps, bt, D_MODEL)
    for j in range(hist):
        nst_ref[j] = window(j + steps)


def _conv_sample(layer, x_tm, st_tm, wpw1, bpw1, wdw, bdw, lng, lnb, wpw2, bpw2, mg, mb):
    steps, bsz, _ = x_tm.shape
    hist = CONV_WIDTH - 1
    return pl.pallas_call(
        functools.partial(_conv_sample_body, layer),
        grid=(bsz // S_BT,),
        in_specs=[pl.BlockSpec((steps, S_BT, D_MODEL), lambda i: (0, i, 0)),
                  pl.BlockSpec((1, hist, S_BT, D_MODEL), lambda i: (layer, 0, i, 0)),
                  _const_spec((D_MODEL, 2 * D_MODEL)), _const_spec((1, 2 * D_MODEL)),
                  _const_spec((CONV_WIDTH, 1, D_MODEL)), _const_spec((1, D_MODEL)),
                  _const_spec((1, D_MODEL)), _const_spec((1, D_MODEL)),
                  _const_spec((D_MODEL, D_MODEL)), _const_spec((1, D_MODEL)),
                  _const_spec((1, D_MODEL)), _const_spec((1, D_MODEL))],
        out_specs=[pl.BlockSpec((steps, S_BT, D_MODEL), lambda i: (0, i, 0)),
                   pl.BlockSpec((hist, S_BT, D_MODEL), lambda i: (0, i, 0))],
        out_shape=[jax.ShapeDtypeStruct((steps, bsz, D_MODEL), F32),
                   jax.ShapeDtypeStruct((hist, bsz, D_MODEL), F32)],
        compiler_params=_params(1),
        name="conv_sample",
    )(x_tm, st_tm, wpw1, bpw1, wdw, bdw, lng, lnb, wpw2, bpw2, mg, mb)


def _headwise_blocks(w):
    per = LANES // ML_QKV_BLOCK
    w4 = w.reshape(-1, per, ML_QKV_BLOCK, ML_QKV_BLOCK)
    eye = jnp.eye(per, dtype=w.dtype)
    return jnp.einsum('jaio,ab->jaibo', w4, eye).reshape(-1, LANES, LANES)


def _headwise_qkv(xc, xm, wqk_ref, wv_ref, j):
    cols = slice(j * LANES, (j + 1) * LANES)
    qk = jnp.dot(xc[:, cols], wqk_ref[j], preferred_element_type=F32)
    v = jnp.dot(xm[:, cols], wv_ref[j], preferred_element_type=F32)
    return qk[:, :LANES], qk[:, LANES:], v


def _gates(q, k, v, wif_ref, bif_ref):
    g = _mm(q, wif_ref[0:D_INNER, :])
    g = g + _mm(k, wif_ref[D_INNER:2 * D_INNER, :])
    g = g + _mm(v, wif_ref[2 * D_INNER:3 * D_INNER, :])
    return g + bif_ref[...]


def _cell_head(qh, qb, kh, vb, igc, lfc, c_old, n_old, m_old):
    length = qh.shape[0]
    row = lax.broadcasted_iota(jnp.int32, (length, length), 0)
    col = lax.broadcasted_iota(jnp.int32, (length, length), 1)
    causal = col <= row
    eye = col == row

    def to_row(c):
        return jnp.sum(jnp.where(eye, c, 0.0), axis=0, keepdims=True)

    lfr = to_row(lfc)
    bc = jnp.sum(jnp.where(causal, lfr, 0.0), axis=1, keepdims=True)
    br = to_row(bc)
    igr = to_row(igc)
    d = jnp.where(causal, bc - br + igr, -jnp.inf)
    inter = bc + m_old
    m_t = jnp.maximum(inter, jnp.max(d, axis=1, keepdims=True))
    s = lax.dot_general(qb, kh.astype(BF16), (((1,), (1,)), ((), ())), preferred_element_type=F32)
    p = s * jnp.exp(d - m_t)
    a = jnp.exp(inter - m_t)
    num = a * jnp.dot(qb, c_old.astype(BF16), preferred_element_type=F32)
    num = num + jnp.dot(p.astype(BF16), vb, preferred_element_type=F32)
    den = a * jnp.sum(qh * n_old, axis=1, keepdims=True) + jnp.sum(p, axis=1, keepdims=True)
    h = num * (1.0 / jnp.maximum(jnp.abs(den), jnp.exp(-m_t)))

    b_last = bc[length - 1:length, :]
    gc = b_last - bc + igc
    m_new = jnp.maximum(b_last + m_old, jnp.max(gc, axis=0, keepdims=True))
    decay = jnp.exp(b_last + m_old - m_new)
    wk = kh * jnp.exp(gc - m_new)
    c_new = decay * c_old + lax.dot_general(wk.astype(BF16), vb, (((0,), (0,)), ((), ())),
                                            preferred_element_type=F32)
    n_new = decay * n_old + jnp.sum(wk, axis=0, keepdims=True)
    return h, c_new, n_new, m_new


def _head_out(h, o_pre, xc, ng, sk):
    mu = jnp.mean(h, axis=-1, keepdims=True)
    hc = h - mu
    var = jnp.mean(hc * hc, axis=-1, keepdims=True)
    hn = hc * lax.rsqrt(var + LN_EPS) * ng
    return jax.nn.sigmoid(o_pre) * (hn + sk * xc)


def _mlstm_prompt_body(n_prev, x_ref, win_ref, cw_ref, cb_ref, wqk_ref, wv_ref, wif_ref,
                       bif_ref, ng_ref, sk_ref, wdn_ref, mg_ref, mb_ref, *rest):
    (y_ref, mlc_ref, c_all_ref, n_all_ref, m_all_ref,
     xbuf_ref, op_ref, xc_ref, q_ref, k_ref, xb_ref, xcb_ref, xmb_ref, qb_ref, kb_ref,
     vb_ref) = rest[n_prev:]
    c_ref, n_ref, m_ref = c_all_ref.at[0], n_all_ref.at[0], m_all_ref.at[0]
    t = pl.program_id(1)
    rows = ML_CHUNK
    hist = ML_CONV_WIDTH - 1

    @pl.when(t == 0)
    def _():
        xbuf_ref[0:ML_HEADROWS, :] = jnp.zeros((ML_HEADROWS, D_INNER), F32)
        c_ref[...] = jnp.zeros(c_ref.shape, F32)
        n_ref[...] = jnp.zeros(n_ref.shape, F32)
        m_ref[...] = jnp.zeros(m_ref.shape, F32)

    x = x_ref[0]
    xb_ref[...] = x.astype(BF16)
    xbuf_ref[ML_HEADROWS:ML_HEADROWS + rows, :] = jnp.dot(xb_ref[...], win_ref[:, :D_INNER],
                                                          preferred_element_type=F32)
    op_ref[...] = jnp.dot(xb_ref[...], win_ref[:, D_INNER:], preferred_element_type=F32)

    for j in range(D_INNER // LANES):
        cols = slice(j * LANES, (j + 1) * LANES)
        for i in range(rows // ML_RCHUNK):
            r0 = i * ML_RCHUNK
            out_rows = slice(r0, r0 + ML_RCHUNK)
            win = xbuf_ref[r0:r0 + ML_RCHUNK + ML_HEADROWS, cols]
            xm = win[ML_HEADROWS:, :]
            acc = xm * cw_ref[hist, :, cols] + cb_ref[:, cols]
            for k in range(hist):
                off = ML_HEADROWS - hist + k
                acc = acc + win[off:off + ML_RCHUNK, :] * cw_ref[k, :, cols]
            xc = acc * jax.nn.sigmoid(acc)
            xc_ref[out_rows, cols] = xc
            xcb_ref[out_rows, cols] = xc.astype(BF16)
            xmb_ref[out_rows, cols] = xm.astype(BF16)
        q, k, v = _headwise_qkv(xcb_ref, xmb_ref, wqk_ref, wv_ref, j)
        q_ref[:, cols] = q
        k_ref[:, cols] = k
        qb_ref[:, cols] = q.astype(BF16)
        kb_ref[:, cols] = k.astype(BF16)
        vb_ref[:, cols] = v.astype(BF16)

    @pl.when(t == pl.num_programs(1) - 1)
    def _():
        mlc_ref[0, 0] = xbuf_ref[ML_HEADROWS + rows - hist:ML_HEADROWS + rows, :]

    xbuf_ref[0:ML_HEADROWS, :] = xbuf_ref[rows:rows + ML_HEADROWS, :]

    gates = jnp.dot(qb_ref[...], wif_ref[0:D_INNER, :], preferred_element_type=F32)
    gates = gates + jnp.dot(kb_ref[...], wif_ref[D_INNER:2 * D_INNER, :], preferred_element_type=F32)
    gates = gates + jnp.dot(vb_ref[...], wif_ref[2 * D_INNER:, :], preferred_element_type=F32)
    gates = gates + bif_ref[...]

    mix = jnp.zeros((rows, D_MODEL), F32)
    for h in range(ML_HEADS):
        cols = slice(h * ML_HEAD_DIM, (h + 1) * ML_HEAD_DIM)
        igc = gates[:, h:h + 1]
        lfc = _log_sigmoid(gates[:, ML_HEADS + h:ML_HEADS + h + 1])
        hh, c_new, n_new, m_new = _cell_head(q_ref[:, cols], qb_ref[:, cols],
                                             k_ref[:, cols] * K_SCALE, vb_ref[:, cols], igc, lfc,
                                             c_ref[0, h], n_ref[0, h:h + 1, :], m_ref[0, :, h:h + 1])
        c_ref[0, h] = c_new
        n_ref[0, h:h + 1, :] = n_new
        m_ref[0, :, h:h + 1] = m_new
        out = _head_out(hh, op_ref[:, cols], xc_ref[:, cols], ng_ref[:, cols], sk_ref[:, cols])
        mix = mix + _mm(out, wdn_ref[cols, :])
    y_ref[0] = _ln(DN_ALPHA * x + mix, mg_ref[...], mb_ref[...])


def _stacked_state(layer, n_layers, prev, shapes, n_inputs, first_out, per_step=1):
    def spec(shape):
        nd = len(shape)
        return pl.BlockSpec((1, per_step) + shape[1:], lambda b, *_: (layer, b) + (0,) * (nd - 1))

    out_specs = [spec(s) for s in shapes]
    out_shapes = [jax.ShapeDtypeStruct((n_layers,) + s, F32) for s in shapes]
    prev_specs = [pl.BlockSpec(memory_space=pl.ANY) for _ in prev]
    aliases = {n_inputs + i: first_out + i for i in range(len(prev))}
    return out_specs, out_shapes, prev_specs, aliases


def _mlstm_prompt(layer, n_layers, prev, x, win, cw, cb, wqk, wv, wif, bif, ng, sk, wdn, mg, mb):
    bsz, seq, _ = x.shape
    rows = ML_CHUNK
    hist = ML_CONV_WIDTH - 1
    tiles = D_INNER // LANES
    weights = (win, cw, cb, wqk, wv, wif, bif, ng, sk, wdn, mg, mb)
    st_specs, st_shapes, prev_specs, aliases = _stacked_state(
        layer, n_layers, prev,
        [(bsz, hist, D_INNER), (bsz, ML_HEADS, ML_HEAD_DIM, ML_HEAD_DIM),
         (bsz, ML_HEADS, ML_HEAD_DIM), (bsz, 1, ML_HEADS)],
        n_inputs=1 + len(weights), first_out=1)
    return pl.pallas_call(
        functools.partial(_mlstm_prompt_body, len(prev)),
        grid=(bsz, seq // rows),
        in_specs=[pl.BlockSpec((1, rows, D_MODEL), lambda b, t: (b, t, 0)),
                  _const_spec((D_MODEL, 2 * D_INNER)),
                  _const_spec((ML_CONV_WIDTH, 1, D_INNER)), _const_spec((1, D_INNER)),
                  _const_spec((tiles, LANES, 2 * LANES)), _const_spec((tiles, LANES, LANES)),
                  _const_spec((3 * D_INNER, GATE_LANES)), _const_spec((1, GATE_LANES)),
                  _const_spec((1, D_INNER)), _const_spec((1, D_INNER)),
                  _const_spec((D_INNER, D_MODEL)),
                  _const_spec((1, D_MODEL)), _const_spec((1, D_MODEL))] + prev_specs,
        out_specs=[pl.BlockSpec((1, rows, D_MODEL), lambda b, t: (b, t, 0))] + st_specs,
        out_shape=[jax.ShapeDtypeStruct((bsz, seq, D_MODEL), F32)] + st_shapes,
        input_output_aliases=aliases,
        scratch_shapes=[pltpu.VMEM((ML_HEADROWS + rows, D_INNER), F32),
                        pltpu.VMEM((rows, D_INNER), F32),
                        pltpu.VMEM((rows, D_INNER), F32),
                        pltpu.VMEM((rows, D_INNER), F32),
                        pltpu.VMEM((rows, D_INNER), F32),
                        pltpu.VMEM((rows, D_MODEL), BF16),
                        pltpu.VMEM((rows, D_INNER), BF16),
                        pltpu.VMEM((rows, D_INNER), BF16),
                        pltpu.VMEM((rows, D_INNER), BF16),
                        pltpu.VMEM((rows, D_INNER), BF16),
                        pltpu.VMEM((rows, D_INNER), BF16)],
        compiler_params=_params(2),
        name="mlstm_prompt",
    )(x, *weights, *prev)


def _mlstm_sample_proj_body(layer, x_ref, st_ref, win_ref, cw_ref, cb_ref, wqk_ref, wv_ref,
                            wif_ref, bif_ref, q_ref, k_ref, v_ref, g_ref, xc_ref, op_ref, nst_ref):
    del layer
    steps, bt, _ = x_ref.shape
    hist = ML_CONV_WIDTH - 1
    x = x_ref[...].reshape(steps * bt, D_MODEL)
    z = _mm(x, win_ref[...])
    xm = z[:, :D_INNER]
    op_ref[...] = z[:, D_INNER:].reshape(steps, bt, D_INNER)

    def window(j):
        if j < hist:
            return st_ref[0, j]
        return xm[(j - hist) * bt:(j - hist + 1) * bt, :]

    xcs = []
    for t in range(steps):
        acc = jnp.zeros((bt, D_INNER), F32) + cb_ref[...]
        for k in range(ML_CONV_WIDTH):
            acc = acc + window(t + k) * cw_ref[k]
        xcs.append(acc * jax.nn.sigmoid(acc))
    xc = jnp.concatenate(xcs, axis=0)
    for j in range(hist):
        nst_ref[j] = window(j + steps)
    xcb = xc.astype(BF16)
    xmb = xm.astype(BF16)
    tiles = [_headwise_qkv(xcb, xmb, wqk_ref, wv_ref, j) for j in range(D_INNER // LANES)]
    q, k, v = (jnp.concatenate(part, axis=1) for part in zip(*tiles))
    gates = _gates(q, k, v, wif_ref, bif_ref)
    q_ref[...] = q.reshape(steps, bt, D_INNER)
    k_ref[...] = k.reshape(steps, bt, D_INNER)
    v_ref[...] = v.reshape(steps, bt, D_INNER)
    g_ref[...] = gates.reshape(steps, bt, GATE_LANES)
    xc_ref[...] = xc.reshape(steps, bt, D_INNER)


def _mlstm_sample_proj(layer, x_tm, st_tm, win, cw, cb, wqk, wv, wif, bif):
    steps, bsz, _ = x_tm.shape
    hist = ML_CONV_WIDTH - 1
    tiles = D_INNER // LANES
    wide = pl.BlockSpec((steps, S_BT, D_INNER), lambda i: (0, i, 0))
    wide_shape = jax.ShapeDtypeStruct((steps, bsz, D_INNER), F32)
    return pl.pallas_call(
        functools.partial(_mlstm_sample_proj_body, layer),
        grid=(bsz // S_BT,),
        in_specs=[pl.BlockSpec((steps, S_BT, D_MODEL), lambda i: (0, i, 0)),
                  pl.BlockSpec((1, hist, S_BT, D_INNER), lambda i: (layer, 0, i, 0)),
                  _const_spec((D_MODEL, 2 * D_INNER)),
                  _const_spec((ML_CONV_WIDTH, 1, D_INNER)), _const_spec((1, D_INNER)),
                  _const_spec((tiles, LANES, 2 * LANES)), _const_spec((tiles, LANES, LANES)),
                  _const_spec((3 * D_INNER, GATE_LANES)), _const_spec((1, GATE_LANES))],
        out_specs=[wide, wide, wide,
                   pl.BlockSpec((steps, S_BT, GATE_LANES), lambda i: (0, i, 0)),
                   wide, wide,
                   pl.BlockSpec((hist, S_BT, D_INNER), lambda i: (0, i, 0))],
        out_shape=[wide_shape, wide_shape, wide_shape,
                   jax.ShapeDtypeStruct((steps, bsz, GATE_LANES), F32),
                   wide_shape, wide_shape,
                   jax.ShapeDtypeStruct((hist, bsz, D_INNER), F32)],
        compiler_params=_params(1),
        name="mlstm_sample_proj",
    )(x_tm, st_tm, win, cw, cb, wqk, wv, wif, bif)


def _mlstm_sample_cell_body(steps, n_prev, q_ref, k_ref, v_ref, g_ref, c_ref, n_ref, m_ref, *rest):
    h_ref, co_ref, no_ref, mo_ref = rest[n_prev:]
    first = (pl.program_id(0) * S_SEQ) % SUBLANES
    sub = lax.broadcasted_iota(jnp.int32, (SUBLANES, 1), 0)
    valid = sub < steps

    @pl.when(first == 0)
    def _():
        h_ref[...] = jnp.zeros(h_ref.shape, F32)

    for s in range(S_SEQ):
        mine = sub == first + s

        def gather(ref, cols):
            out = jnp.zeros((S_TPAD, cols.stop - cols.start), F32)
            for t in range(steps):
                row = jnp.sum(jnp.where(mine, ref[t, :, cols], 0.0), axis=0, keepdims=True)
                out = jnp.where(sub == t, row, out)
            return out

        gates = gather(g_ref, slice(0, GATE_LANES))
        for h in range(ML_HEADS):
            cols = slice(h * ML_HEAD_DIM, (h + 1) * ML_HEAD_DIM)
            igc = jnp.where(valid, gates[:, h:h + 1], -jnp.inf)
            lfc = jnp.where(valid, _log_sigmoid(gates[:, ML_HEADS + h:ML_HEADS + h + 1]), 0.0)
            qh = gather(q_ref, cols)
            hh, c_new, n_new, m_new = _cell_head(qh, qh.astype(BF16), gather(k_ref, cols) * K_SCALE,
                                                 gather(v_ref, cols).astype(BF16), igc, lfc,
                                                 c_ref[0, s, h], n_ref[0, s, h:h + 1, :],
                                                 m_ref[0, s, :, h:h + 1])
            for t in range(steps):
                h_ref[t, :, cols] = jnp.where(mine, hh[t:t + 1, :], h_ref[t, :, cols])
            co_ref[0, s, h] = c_new
            no_ref[0, s, h:h + 1, :] = n_new
            mo_ref[0, s, :, h:h + 1] = m_new


def _mlstm_sample_cell(layer, n_layers, prev, q_tm, k_tm, v_tm, g_tm, c_st, n_st, m_st):
    steps, bsz, _ = q_tm.shape
    tile_of = lambda b: (0, (b * S_SEQ) // SUBLANES, 0)
    wide = pl.BlockSpec((steps, SUBLANES, D_INNER), tile_of)
    st_specs, st_shapes, prev_specs, aliases = _stacked_state(
        layer, n_layers, prev,
        [(bsz, ML_HEADS, ML_HEAD_DIM, ML_HEAD_DIM), (bsz, ML_HEADS, ML_HEAD_DIM),
         (bsz, 1, ML_HEADS)],
        n_inputs=7, first_out=1, per_step=S_SEQ)
    return pl.pallas_call(
        functools.partial(_mlstm_sample_cell_body, steps, len(prev)),
        grid=(bsz // S_SEQ,),
        in_specs=[wide, wide, wide,
                  pl.BlockSpec((steps, SUBLANES, GATE_LANES), tile_of),
                  pl.BlockSpec((1, S_SEQ, ML_HEADS, ML_HEAD_DIM, ML_HEAD_DIM),
                               lambda b: (layer, b, 0, 0, 0)),
                  pl.BlockSpec((1, S_SEQ, ML_HEADS, ML_HEAD_DIM), lambda b: (layer, b, 0, 0)),
                  pl.BlockSpec((1, S_SEQ, 1, ML_HEADS), lambda b: (layer, b, 0, 0))] + prev_specs,
        out_specs=[wide] + st_specs,
        out_shape=[jax.ShapeDtypeStruct((steps, bsz, D_INNER), F32)] + st_shapes,
        input_output_aliases=aliases,
        compiler_params=_params(1),
        name="mlstm_sample_cell",
    )(q_tm, k_tm, v_tm, g_tm, c_st, n_st, m_st, *prev)


def _mlstm_sample_out_body(x_ref, h_ref, xc_ref, op_ref, ng_ref, sk_ref, wdn_ref, mg_ref, mb_ref,
                           y_ref):
    steps, bt, _ = x_ref.shape
    n = steps * bt
    x = x_ref[...].reshape(n, D_MODEL)
    mix = jnp.zeros((n, D_MODEL), F32)
    for h in range(ML_HEADS):
        cols = slice(h * ML_HEAD_DIM, (h + 1) * ML_HEAD_DIM)
        out = _head_out(h_ref[:, :, cols].reshape(n, ML_HEAD_DIM),
                        op_ref[:, :, cols].reshape(n, ML_HEAD_DIM),
                        xc_ref[:, :, cols].reshape(n, ML_HEAD_DIM),
                        ng_ref[:, cols], sk_ref[:, cols])
        mix = mix + _mm(out, wdn_ref[cols, :])
    y_ref[...] = _ln(DN_ALPHA * x + mix, mg_ref[...], mb_ref[...]).reshape(steps, bt, D_MODEL)


def _mlstm_sample_out(x_tm, h_tm, xc_tm, op_tm, ng, sk, wdn, mg, mb):
    steps, bsz, _ = x_tm.shape
    wide = pl.BlockSpec((steps, S_BT, D_INNER), lambda i: (0, i, 0))
    narrow = pl.BlockSpec((steps, S_BT, D_MODEL), lambda i: (0, i, 0))
    return pl.pallas_call(
        _mlstm_sample_out_body,
        grid=(bsz // S_BT,),
        in_specs=[narrow, wide, wide, wide,
                  _const_spec((1, D_INNER)), _const_spec((1, D_INNER)),
                  _const_spec((D_INNER, D_MODEL)),
                  _const_spec((1, D_MODEL)), _const_spec((1, D_MODEL))],
        out_specs=narrow,
        out_shape=jax.ShapeDtypeStruct((steps, bsz, D_MODEL), F32),
        compiler_params=_params(1),
        name="mlstm_sample_out",
    )(x_tm, h_tm, xc_tm, op_tm, ng, sk, wdn, mg, mb)


def _row(v):
    return v.reshape(1, -1)


def kernel(x_prompt, x_sample, state_conv, state_mlstm_conv, state_mlstm_C, state_mlstm_n, state_mlstm_m, conv_w_pw1, conv_b_pw1, conv_w_dw, conv_b_dw, conv_ln_g, conv_ln_b, conv_w_pw2, conv_b_pw2, ml_w_in, ml_conv_w, ml_conv_b, ml_w_q, ml_w_k, ml_w_v, ml_w_if, ml_b_if, ml_norm_g, ml_skip, ml_w_down, ln_mix_g, ln_mix_b, ffn_w1, ffn_w2, ln_ffn_g, ln_ffn_b):
    bsz, seq, _ = x_prompt.shape
    dbs, steps, _ = x_sample.shape

    def per_layer(w):
        return [w[layer].astype(BF16) for layer in range(w.shape[0])]

    wpw1, wpw2, win, wdn = (per_layer(w) for w in (conv_w_pw1, conv_w_pw2, ml_w_in, ml_w_down))
    w1, w2 = per_layer(ffn_w1), per_layer(ffn_w2)
    wif = per_layer(jnp.pad(ml_w_if, ((0, 0), (0, 0), (0, GATE_LANES - 2 * ML_HEADS))))
    bif = jnp.pad(ml_b_if, ((0, 0), (0, GATE_LANES - 2 * ML_HEADS)))

    xs = jnp.transpose(x_sample, (1, 0, 2))
    conv_st_tm = jnp.transpose(state_conv, (0, 2, 1, 3))
    mlc_st_tm = jnp.transpose(state_mlstm_conv, (0, 2, 1, 3))
    m_st = state_mlstm_m.reshape(state_mlstm_m.shape[0], dbs, 1, ML_HEADS)

    xp = x_prompt
    n_ml = DEPTH // 2
    conv_p, conv_s, mlc_s = [], [], []
    ml_p, ml_s = (), ()
    for i in range(DEPTH):
        j = i // 2
        mg, mb = _row(ln_mix_g[i]), _row(ln_mix_b[i])
        fg, fb = _row(ln_ffn_g[i]), _row(ln_ffn_b[i])
        if i % 2 == 0:
            taps = conv_w_dw[j][:, None, :]
            args = (wpw1[j], _row(conv_b_pw1[j]), taps, _row(conv_b_dw[j]),
                    _row(conv_ln_g[j]), _row(conv_ln_b[j]), wpw2[j], _row(conv_b_pw2[j]), mg, mb)
            wide_taps = jnp.broadcast_to(taps, (CONV_WIDTH, SUBLANES, D_MODEL))
            xp, st = _conv_ffn_prompt(xp, *args[:2], wide_taps, *args[3:], w1[i], w2[i], fg, fb)
            conv_p.append(st)
            xs, st = _conv_sample(j, xs, conv_st_tm, *args)
            conv_s.append(jnp.transpose(st, (1, 0, 2)))
        else:
            wqk = jnp.concatenate([_headwise_blocks(ml_w_q[j]), _headwise_blocks(ml_w_k[j])],
                                  axis=-1).astype(BF16)
            wv = _headwise_blocks(ml_w_v[j]).astype(BF16)
            front = (win[j], ml_conv_w[j][:, None, :], _row(ml_conv_b[j]), wqk, wv, wif[j],
                     _row(bif[j]))
            back = (_row(ml_norm_g[j]), _row(ml_skip[j]), wdn[j], mg, mb)
            xp, *ml_p = _mlstm_prompt(j, n_ml, ml_p, xp, *front, *back)

            q, k, v, g, xc, op, st = _mlstm_sample_proj(j, xs, mlc_st_tm, *front)
            mlc_s.append(jnp.transpose(st, (1, 0, 2)))

            h_tm, *ml_s = _mlstm_sample_cell(j, n_ml, ml_s, q, k, v, g,
                                             state_mlstm_C, state_mlstm_n, m_st)
            xs = _mlstm_sample_out(xs, h_tm, xc, op, *back)
            xp = _ffn(xp.reshape(bsz * seq, D_MODEL), w1[i], w2[i], fg, fb).reshape(bsz, seq, D_MODEL)

        xs = _ffn(xs.reshape(steps * dbs, D_MODEL), w1[i], w2[i], fg, fb).reshape(steps, dbs, D_MODEL)

    y_sample = jnp.transpose(xs, (1, 0, 2))
    mlc_p, c_p, n_p, m_p = ml_p
    c_s, n_s, m_s = ml_s
    return (xp, y_sample, jnp.stack(conv_p), jnp.stack(conv_s), mlc_p, jnp.stack(mlc_s),
            c_p, c_s, n_p, n_s, m_p.reshape(n_ml, bsz, ML_HEADS), m_s.reshape(n_ml, dbs, ML_HEADS))
```
